```python
import jax, jax.numpy as jnp
from jax import lax
import numpy as np

D_MODEL = 1024
BATCH = 16
SEQ = 2048
DEPTH = 2

CHUNK = 64
D_MIX = D_MODEL
HG_WIDTH = D_MIX // 2
HG_EXPAND = 128
HG_HEADS = HG_WIDTH // HG_EXPAND
HG_DK = HG_EXPAND
HG_DV = HG_WIDTH // HG_HEADS
GM_WIDTH = D_MIX - HG_WIDTH
GM_HEADS = 4
GM_DH = GM_WIDTH // GM_HEADS
GM_BLOCK = 128
D_FF = 2816
N_MOD = 9
RMS_EPS = 1e-6
LN_EPS = 1e-5
IN_COLS = 4 * HG_WIDTH + 2 * GM_WIDTH

kernel_name = "hybrid_hgrn2_gmlp_macaron_adaln"


def rms_norm(x, gain, eps=RMS_EPS):
    xf = x.astype(jnp.float32)
    y = xf * lax.rsqrt(jnp.mean(xf * xf, axis=-1, keepdims=True) + eps)
    return (y * gain.astype(jnp.float32)).astype(x.dtype)


def modulate(x, shift, scale):
    return x * (1 + scale[:, None, :]) + shift[:, None, :]


def swiglu_ffn(x, w13, w2):
    a, b = jnp.split(x @ w13, 2, axis=-1)
    return (jax.nn.silu(a) * b) @ w2


def hgrn2_mixer(q, f_raw, i, g, lb, gnorm_gain):
    B, L, _ = q.shape
    nC = L // CHUNK
    dt = q.dtype
    lbf = lb.astype(jnp.float32)
    f = lbf + (1 - lbf) * jax.nn.sigmoid(f_raw.astype(jnp.float32))
    log_f = jnp.log(f)
    k = 1.0 - f

    def to_chunks(t, d):
        return t.astype(jnp.float32).reshape(B, nC, CHUNK, HG_HEADS, d).transpose(1, 0, 3, 2, 4)

    qc = to_chunks(q, HG_DK)
    kc = to_chunks(k, HG_DK)
    vc = to_chunks(i, HG_DV)
    bc = jnp.cumsum(to_chunks(log_f, HG_DK), axis=3)
    tri = jnp.tril(jnp.ones((CHUNK, CHUNK), dtype=bool))[:, :, None]

    def step(S, inp):
        qt, kt, vt, bt = inp
        diff = bt[:, :, :, None, :] - bt[:, :, None, :, :]
        decay = jnp.where(tri, jnp.exp(jnp.where(tri, diff, 0.0)), 0.0)
        A = jnp.einsum('bhtk,bhtsk,bhsk->bhts', qt, decay, kt)
        o = (jnp.einsum('bhts,bhsv->bhtv', A, vt)
             + jnp.einsum('bhtk,bhkv->bhtv', qt * jnp.exp(bt), S))
        b_last = bt[:, :, -1:, :]
        S = (jnp.exp(b_last[:, :, 0, :])[..., None] * S
             + jnp.einsum('bhsk,bhsv->bhkv', kt * jnp.exp(b_last - bt), vt))
        return S, o

    S0 = jnp.zeros((B, HG_HEADS, HG_DK, HG_DV), jnp.float32)
    _, o = lax.scan(step, S0, (qc, kc, vc, bc))
    o = o.transpose(1, 0, 3, 2, 4).reshape(B, L, HG_HEADS, HG_DV)
    o = rms_norm(o, gnorm_gain.reshape(HG_HEADS, HG_DV))
    o = o.reshape(B, L, HG_WIDTH) * jax.nn.silu(g.astype(jnp.float32))
    return o.astype(dt)


def gmlp_mixer(u, v, ln_gain, w_sp, b_sp, out_gain):
    B, L, _ = u.shape
    nB = L // GM_BLOCK
    dt = u.dtype
    u = jax.nn.gelu(u).reshape(B, nB, GM_BLOCK, GM_HEADS, GM_DH)
    v = jax.nn.gelu(v).reshape(B, nB, GM_BLOCK, GM_HEADS, GM_DH)
    vf = v.astype(jnp.float32)
    mu = jnp.mean(vf, axis=-1, keepdims=True)
    var = jnp.mean(jnp.square(vf - mu), axis=-1, keepdims=True)
    vn = ((vf - mu) * lax.rsqrt(var + LN_EPS) * ln_gain.reshape(GM_HEADS, GM_DH).astype(jnp.float32)).astype(dt)
    cpos = jnp.arange(GM_BLOCK) // CHUNK
    mask = cpos[:, None] >= cpos[None, :]
    w = jnp.where(mask[None], w_sp, 0.0)
    mixed = (jnp.einsum('hts,bnshd->bnthd', w, vn)
             + b_sp.T[None, None, :, :, None])
    y = rms_norm(u * mixed, out_gain.reshape(GM_HEADS, GM_DH))
    return y.reshape(B, L, GM_WIDTH)


def setup_inputs(seed: int = 0) -> dict:
    key = jax.random.key(seed)
    ks = jax.random.split(key, 18)

    def nrm(k, shape, scale):
        return scale * jax.random.normal(k, shape, jnp.float32)

    return {
        "x": nrm(ks[0], (BATCH, SEQ, D_MODEL), 1.0),
        "c": nrm(ks[1], (BATCH, D_MODEL), 1.0),
        "w_ada": nrm(ks[2], (DEPTH, D_MODEL, N_MOD * D_MODEL), 0.5 * D_MODEL ** -0.5),
        "b_ada": nrm(ks[3], (DEPTH, N_MOD * D_MODEL), 0.02),
        "norm_gain": 1.0 + nrm(ks[4], (DEPTH, 3, D_MODEL), 0.05),
        "ffn1_w13": nrm(ks[5], (DEPTH, D_MODEL, 2 * D_FF), D_MODEL ** -0.5),
        "ffn1_w2": nrm(ks[6], (DEPTH, D_FF, D_MODEL), D_FF ** -0.5),
        "w_in": nrm(ks[7], (DEPTH, D_MODEL, IN_COLS), D_MODEL ** -0.5),
        "hg_lb_logits": nrm(ks[8], (DEPTH, HG_WIDTH), 0.5),
        "hg_gnorm": 1.0 + nrm(ks[9], (DEPTH, HG_WIDTH), 0.05),
        "gm_ln_gain": 1.0 + nrm(ks[10], (DEPTH, GM_WIDTH), 0.05),
        "gm_w_spatial": nrm(ks[11], (DEPTH, GM_HEADS, GM_BLOCK, GM_BLOCK), 0.5 * GM_BLOCK ** -0.5),
        "gm_b_spatial": 1.0 + nrm(ks[12], (DEPTH, GM_HEADS, GM_BLOCK), 0.1),
        "gm_out_gain": 1.0 + nrm(ks[13], (DEPTH, GM_WIDTH), 0.05),
        "w_out": nrm(ks[14], (DEPTH, D_MIX, D_MODEL), D_MIX ** -0.5),
        "ffn2_w13": nrm(ks[15], (DEPTH, D_MODEL, 2 * D_FF), D_MODEL ** -0.5),
        "ffn2_w2": nrm(ks[16], (DEPTH, D_FF, D_MODEL), D_FF ** -0.5),
        "final_gain": 1.0 + nrm(ks[17], (D_MODEL,), 0.05),
    }


def reference(x, c, w_ada, b_ada, norm_gain, ffn1_w13, ffn1_w2, w_in, hg_lb_logits, hg_gnorm,
              gm_ln_gain, gm_w_spatial, gm_b_spatial, gm_out_gain, w_out, ffn2_w13, ffn2_w2,
              final_gain):
    B = x.shape[0]
    lb_all = jnp.cumsum(jax.nn.softmax(hg_lb_logits.astype(jnp.float32), axis=0), axis=0)
    lb_all = lb_all - lb_all[0]
    splits = [HG_WIDTH, 2 * HG_WIDTH, 3 * HG_WIDTH, 4 * HG_WIDTH, 4 * HG_WIDTH + GM_WIDTH]
    c_act = jax.nn.silu(c)
    h = x
    for l in range(DEPTH):
        mod = (c_act @ w_ada[l] + b_ada[l]).reshape(B, N_MOD, D_MODEL)
        sh1, sc1, g1 = mod[:, 0], mod[:, 1], mod[:, 2]
        sh2, sc2, g2 = mod[:, 3], mod[:, 4], mod[:, 5]
        sh3, sc3, g3 = mod[:, 6], mod[:, 7], mod[:, 8]

        y = modulate(rms_norm(h, norm_gain[l, 0]), sh1, sc1)
        h = h + 0.5 * g1[:, None, :] * swiglu_ffn(y, ffn1_w13[l], ffn1_w2[l])

        y = modulate(rms_norm(h, norm_gain[l, 1]), sh2, sc2)
        proj = y @ w_in[l]
        q, f_raw, i_in, g_out, u, v = jnp.split(proj, splits, axis=-1)
        o_hg = hgrn2_mixer(q, f_raw, i_in, g_out, lb_all[l], hg_gnorm[l])
        o_gm = gmlp_mixer(u, v, gm_ln_gain[l], gm_w_spatial[l], gm_b_spatial[l], gm_out_gain[l])
        mix = jnp.concatenate([o_hg, o_gm], axis=-1) @ w_out[l]
        h = h + g2[:, None, :] * mix

        y = modulate(rms_norm(h, norm_gain[l, 2]), sh3, sc3)
        h = h + 0.5 * g3[:, None, :] * swiglu_ffn(y, ffn2_w13[l], ffn2_w2[l])

    return rms_norm(h, final_gain)
```

```python
import functools

import numpy as np
import jax
import jax.numpy as jnp
from jax import lax
from jax.experimental import pallas as pl
from jax.experimental.pallas import tpu as pltpu

D_MODEL = 1024
SEQ = 2048
CHUNK = 64
HG_WIDTH = 512
HG_HEADS = 4
HG_DK = 128
GM_WIDTH = 512
GM_HEADS = 4
GM_DH = 128
GM_BLOCK = 128
D_FF = 2816
N_MOD = 9
IN_COLS = 4 * HG_WIDTH + 2 * GM_WIDTH
RMS_EPS = 1e-6
LN_EPS = 1e-5

FF_CHUNK = 256
N_FF_CHUNKS = D_FF // FF_CHUNK
FFN_ROWS = 512
MIX_ROWS = 512
ADA_COLS = 1024
HALF_SIZES = (32, 16, 8, 4, 2, 1)
N_LEVELS = len(HALF_SIZES)
VMEM_LIMIT = 56 * 1024 * 1024

F32 = jnp.float32
BF16 = jnp.bfloat16


def _rms(x, eps):
    return x * lax.rsqrt(jnp.mean(x * x, axis=-1, keepdims=True) + eps)


def _resident(shape):
    return pl.BlockSpec(shape, lambda *_: (0,) * len(shape), pipeline_mode=pl.Buffered(1))


def _ada_kernel(c_ref, w_ref, b_ref, o_ref):
    c = c_ref[...]
    ca = c * jax.nn.sigmoid(c)
    o_ref[0] = jnp.dot(ca, w_ref[0], precision=lax.Precision.HIGHEST,
                       preferred_element_type=F32) + b_ref[0]


def _ada_call(c, w_ada, b_ada):
    depth, _, n_out = w_ada.shape
    batch = c.shape[0]
    return pl.pallas_call(
        _ada_kernel,
        grid=(depth, n_out // ADA_COLS),
        in_specs=[
            pl.BlockSpec((batch, D_MODEL), lambda l, j: (0, 0)),
            pl.BlockSpec((1, D_MODEL, ADA_COLS), lambda l, j: (l, 0, j)),
            pl.BlockSpec((1, 1, ADA_COLS), lambda l, j: (l, 0, j)),
        ],
        out_specs=pl.BlockSpec((1, batch, ADA_COLS), lambda l, j: (l, 0, j)),
        out_shape=jax.ShapeDtypeStruct((depth, batch, n_out), F32),
        compiler_params=pltpu.CompilerParams(
            dimension_semantics=("arbitrary", "arbitrary"), vmem_limit_bytes=VMEM_LIMIT),
        name="ada",
    )(c, w_ada, b_ada.reshape(depth, 1, n_out))


def _ffn_kernel(h_ref, mod_ref, ng_ref, w13_ref, w2_ref, fg_ref, o_ref, y_ref, acc_ref,
                *, mod_base, final):
    h = h_ref[...]
    sh = mod_ref[0, mod_base:mod_base + 1, :]
    sc = mod_ref[0, mod_base + 1:mod_base + 2, :]
    gate = mod_ref[0, mod_base + 2:mod_base + 3, :]
    y = _rms(h, RMS_EPS) * ng_ref[...]
    y_ref[...] = (y * (1.0 + sc) + sh).astype(BF16)
    acc_ref[...] = jnp.zeros_like(acc_ref)

    def body(j, carry):
        ab = jnp.dot(y_ref[...], w13_ref[j], preferred_element_type=F32)
        a = ab[:, :FF_CHUNK]
        b = ab[:, FF_CHUNK:]
        hm = (a * jax.nn.sigmoid(a) * b).astype(BF16)
        acc_ref[...] += jnp.dot(hm, w2_ref[j], preferred_element_type=F32)
        return carry

    lax.fori_loop(0, N_FF_CHUNKS, body, 0)
    res = h_ref[...] + 0.5 * gate * acc_ref[...]
    if final:
        res = _rms(res, RMS_EPS) * fg_ref[...]
    o_ref[...] = res


def _ffn_call(h2d, mod, ngain, w13c, w2c, fgain, *, mod_base, final):
    n_tok = h2d.shape[0]
    tiles_per_seq = SEQ // FFN_ROWS
    kern = functools.partial(_ffn_kernel, mod_base=mod_base, final=final)
    return pl.pallas_call(
        kern,
        grid=(n_tok // FFN_ROWS,),
        in_specs=[
            pl.BlockSpec((FFN_ROWS, D_MODEL), lambda i: (i, 0)),
            pl.BlockSpec((1, N_MOD, D_MODEL), lambda i: (i // tiles_per_seq, 0, 0)),
            _resident((1, D_MODEL)),
            _resident((N_FF_CHUNKS, D_MODEL, 2 * FF_CHUNK)),
            _resident((N_FF_CHUNKS, FF_CHUNK, D_MODEL)),
            _resident((1, D_MODEL)),
        ],
        out_specs=pl.BlockSpec((FFN_ROWS, D_MODEL), lambda i: (i, 0)),
        out_shape=jax.ShapeDtypeStruct(h2d.shape, F32),
        scratch_shapes=[pltpu.VMEM((FFN_ROWS, D_MODEL), BF16),
                        pltpu.VMEM((FFN_ROWS, D_MODEL), F32)],
        compiler_params=pltpu.CompilerParams(
            dimension_semantics=("arbitrary",), vmem_limit_bytes=VMEM_LIMIT),
        name="ffn_final" if final else "ffn",
    )(h2d, mod, ngain, w13c, w2c, fgain)


def _hgrn_constants():
    r = np.arange(CHUNK)
    sums = np.zeros((N_LEVELS + 2, CHUNK, CHUNK), np.float32)
    pair = np.zeros((N_LEVELS + 1, CHUNK, CHUNK), np.float32)
    sums[0] = r[None, :] <= r[:, None]
    for i, m in enumerate(HALF_SIZES):
        blk = r // (2 * m)
        low = (r % (2 * m)) < m
        bnd = blk * 2 * m + m - 1
        for t in range(CHUNK):
            if low[t]:
                sums[1 + i, t, t + 1:bnd[t] + 1] = 1.0
            else:
                sums[1 + i, t, bnd[t] + 1:t + 1] = 1.0
        pair[i] = (blk[:, None] == blk[None, :]) & (~low)[:, None] & low[None, :]
    pair[N_LEVELS] = np.eye(CHUNK)
    sums[N_LEVELS + 1] = r[None, :] > r[:, None]
    return sums.reshape((N_LEVELS + 2) * CHUNK, CHUNK), pair


def _select_rows(m, upper_val, lower_val):
    rows = upper_val.shape[0]
    if m >= 8:
        parts = []
        for r0 in range(0, rows, m):
            src = upper_val if (r0 // m) % 2 == 1 else lower_val
            parts.append(src[r0:r0 + m])
        return jnp.concatenate(parts, axis=0)
    row = lax.broadcasted_iota(jnp.int32, upper_val.shape, 0)
    return jnp.where((row & m) != 0, upper_val, lower_val)


def _mixer_kernel(h_ref, mod_ref, ng_ref, win_ref, wout_ref, lbl_ref, gn_ref, lng_ref, og_ref,
                  wsp_ref, bsp_ref, sums_ref, pair_ref, o_ref, st_ref, proj_ref, cat_ref,
                  *, layer):
    @pl.when(pl.program_id(1) == 0)
    def _():
        st_ref[...] = jnp.zeros_like(st_ref)

    h = h_ref[0]
    sh = mod_ref[0, 3:4, :]
    sc = mod_ref[0, 4:5, :]
    gate = mod_ref[0, 5:6, :]
    y = _rms(h, RMS_EPS) * ng_ref[...]
    y = (y * (1.0 + sc) + sh).astype(BF16)
    proj_ref[...] = jnp.dot(y, win_ref[...], preferred_element_type=F32)

    lg = lbl_ref[...]
    ex = jnp.exp(lg - jnp.max(lg, axis=0, keepdims=True))
    prob = ex / jnp.sum(ex, axis=0, keepdims=True)
    lb = jnp.zeros((1, HG_WIDTH), F32)
    for i in range(1, layer + 1):
        lb = lb + prob[i:i + 1, :]

    r2 = lax.broadcasted_iota(jnp.int32, (GM_BLOCK, GM_BLOCK), 0)
    c2 = lax.broadcasted_iota(jnp.int32, (GM_BLOCK, GM_BLOCK), 1)
    causal = (r2 // CHUNK) >= (c2 // CHUNK)
    w_sp = [jnp.where(causal, wsp_ref[hd], 0.0).astype(BF16) for hd in range(GM_HEADS)]

    def gm_body(blk, carry):
        r0 = pl.multiple_of(blk * GM_BLOCK, GM_BLOCK)
        rows = pl.ds(r0, GM_BLOCK)
        for hd in range(GM_HEADS):
            cu = 4 * HG_WIDTH + hd * GM_DH
            cv = cu + GM_WIDTH
            u = jax.nn.gelu(proj_ref[rows, cu:cu + GM_DH])
            v = jax.nn.gelu(proj_ref[rows, cv:cv + GM_DH])
            vc = v - jnp.mean(v, axis=-1, keepdims=True)
            var = jnp.mean(vc * vc, axis=-1, keepdims=True)
            vn = vc * lax.rsqrt(var + LN_EPS) * lng_ref[:, hd * GM_DH:(hd + 1) * GM_DH]
            mixed = jnp.dot(w_sp[hd], vn.astype(BF16), preferred_element_type=F32) + bsp_ref[hd]
            z = _rms(u * mixed, RMS_EPS) * og_ref[:, hd * GM_DH:(hd + 1) * GM_DH]
            cat_ref[rows, HG_WIDTH + hd * GM_DH:HG_WIDTH + (hd + 1) * GM_DH] = z.astype(BF16)
        return carry

    lax.fori_loop(0, MIX_ROWS // GM_BLOCK, gm_body, 0)

    def hg_body(ci, carry):
        r0 = pl.multiple_of(ci * CHUNK, CHUNK)
        rows = pl.ds(r0, CHUNK)
        q = proj_ref[rows, 0:HG_WIDTH]
        f_raw = proj_ref[rows, HG_WIDTH:2 * HG_WIDTH]
        f = lb + (1.0 - lb) * jax.nn.sigmoid(f_raw)
        log_f = jnp.log(f)
        k = 1.0 - f
        hi = log_f.astype(BF16)
        mid = (log_f - hi.astype(F32)).astype(BF16)
        sums = (jnp.dot(sums_ref[...], hi, preferred_element_type=F32)
                + jnp.dot(sums_ref[...], mid, preferred_element_type=F32))
        decay_in = jnp.exp(sums[0:CHUNK])
        decay_out = jnp.exp(sums[(N_LEVELS + 1) * CHUNK:(N_LEVELS + 2) * CHUNK])
        lvl = []
        for i, m in enumerate(HALF_SIZES):
            e = jnp.exp(sums[(1 + i) * CHUNK:(2 + i) * CHUNK])
            lvl.append((_select_rows(m, q, k) * e).astype(BF16))
        qb = q.astype(BF16)
        kb = k.astype(BF16)
        q_in = (q * decay_in).astype(BF16)
        k_out = (k * decay_out).astype(BF16)
        for hd in range(HG_HEADS):
            cs = slice(hd * HG_DK, (hd + 1) * HG_DK)
            dn = (((1,), (1,)), ((), ()))
            att = lax.dot_general(qb[:, cs], kb[:, cs], dn, preferred_element_type=F32) * pair_ref[N_LEVELS]
            for i in range(N_LEVELS):
                mh = lvl[i][:, cs]
                att = att + lax.dot_general(mh, mh, dn, preferred_element_type=F32) * pair_ref[i]
            vb = proj_ref[rows, 2 * HG_WIDTH + hd * HG_DK:2 * HG_WIDTH + (hd + 1) * HG_DK].astype(BF16)
            st = st_ref[hd]
            o = (jnp.dot(att.astype(BF16), vb, preferred_element_type=F32)
                 + lax.dot_general(q_in[:, cs], st.astype(BF16), dn, preferred_element_type=F32))
            st_ref[hd] = (st * decay_in[CHUNK - 1:CHUNK, cs]
                          + lax.dot_general(vb, k_out[:, cs], (((0,), (0,)), ((), ())),
                                            preferred_element_type=F32))
            g = proj_ref[rows, 3 * HG_WIDTH + hd * HG_DK:3 * HG_WIDTH + (hd + 1) * HG_DK]
            o = _rms(o, RMS_EPS) * gn_ref[:, cs] * (g * jax.nn.sigmoid(g))
            cat_ref[rows, cs] = o.astype(BF16)
        return carry

    lax.fori_loop(0, MIX_ROWS // CHUNK, hg_body, 0)

    mix = jnp.dot(cat_ref[...], wout_ref[...], preferred_element_type=F32)
    o_ref[0] = h_ref[0] + gate * mix


def _mixer_call(h, mod, ngain, w_in, w_out, lb_logits, gnorm, ln_gain, out_gain, w_sp, b_sp,
                sums_c, pair_c, *, layer):
    batch = h.shape[0]
    depth = lb_logits.shape[0]
    kern = functools.partial(_mixer_kernel, layer=layer)
    return pl.pallas_call(
        kern,
        grid=(batch, SEQ // MIX_ROWS),
        in_specs=[
            pl.BlockSpec((1, MIX_ROWS, D_MODEL), lambda b, j: (b, j, 0)),
            pl.BlockSpec((1, N_MOD, D_MODEL), lambda b, j: (b, 0, 0)),
            _resident((1, D_MODEL)),
            _resident((D_MODEL, IN_COLS)),
            _resident((D_MODEL, D_MODEL)),
            _resident((depth, HG_WIDTH)),
            _resident((1, HG_WIDTH)),
            _resident((1, GM_WIDTH)),
            _resident((1, GM_WIDTH)),
            _resident((GM_HEADS, GM_BLOCK, GM_BLOCK)),
            _resident((GM_HEADS, GM_BLOCK, 1)),
            _resident(sums_c.shape),
            _resident(pair_c.shape),
        ],
        out_specs=pl.BlockSpec((1, MIX_ROWS, D_MODEL), lambda b, j: (b, j, 0)),
        out_shape=jax.ShapeDtypeStruct(h.shape, F32),
        scratch_shapes=[pltpu.VMEM((HG_HEADS, HG_DK, HG_DK), F32),
                        pltpu.VMEM((MIX_ROWS, IN_COLS), F32),
                        pltpu.VMEM((MIX_ROWS, D_MODEL), BF16)],
        compiler_params=pltpu.CompilerParams(
            dimension_semantics=("arbitrary", "arbitrary"), vmem_limit_bytes=VMEM_LIMIT),
        name="mixer",
    )(h, mod, ngain, w_in, w_out, lb_logits, gnorm, ln_gain, out_gain, w_sp, b_sp, sums_c, pair_c)


def _ffn_weights(w13, w2):
    w1 = w13[:, :D_FF].reshape(D_MODEL, N_FF_CHUNKS, FF_CHUNK)
    w3 = w13[:, D_FF:].reshape(D_MODEL, N_FF_CHUNKS, FF_CHUNK)
    w13c = jnp.concatenate([w1, w3], axis=2).transpose(1, 0, 2).astype(BF16)
    w2c = w2.reshape(N_FF_CHUNKS, FF_CHUNK, D_MODEL).astype(BF16)
    return w13c, w2c


def kernel(x, c, w_ada, b_ada, norm_gain, ffn1_w13, ffn1_w2, w_in, hg_lb_logits, hg_gnorm,
           gm_ln_gain, gm_w_spatial, gm_b_spatial, gm_out_gain, w_out, ffn2_w13, ffn2_w2,
           final_gain):
    batch, seq, _ = x.shape
    depth = w_ada.shape[0]
    sums_np, pair_np = _hgrn_constants()
    sums_c = jnp.asarray(sums_np, BF16)
    pair_c = jnp.asarray(pair_np, F32)
    fgain = final_gain.reshape(1, D_MODEL)

    mod_all = _ada_call(c, w_ada, b_ada).reshape(depth, batch, N_MOD, D_MODEL)
    h = x
    for l in range(depth):
        mod = mod_all[l]
        w13c, w2c = _ffn_weights(ffn1_w13[l], ffn1_w2[l])
        h = _ffn_call(h.reshape(batch * seq, D_MODEL), mod, norm_gain[l, 0:1], w13c, w2c, fgain,
                      mod_base=0, final=False).reshape(batch, seq, D_MODEL)
        h = _mixer_call(h, mod, norm_gain[l, 1:2], w_in[l].astype(BF16), w_out[l].astype(BF16),
                        hg_lb_logits, hg_gnorm[l:l + 1], gm_ln_gain[l:l + 1], gm_out_gain[l:l + 1],
                        gm_w_spatial[l], gm_b_spatial[l].reshape(GM_HEADS, GM_BLOCK, 1),
                        sums_c, pair_c, layer=l)
        w13c, w2c = _ffn_weights(ffn2_w13[l], ffn2_w2[l])
        h = _ffn_call(h.reshape(batch * seq, D_MODEL), mod, norm_gain[l, 2:3], w13c, w2c, fgain,
                      mod_base=6, final=(l == depth - 1)).reshape(batch, seq, D_MODEL)
    return h
```

```python
import functools

import numpy as np
import jax
import jax.numpy as jnp
from jax import lax
from jax.experimental import pallas as pl
from jax.experimental.pallas import tpu as pltpu

D_MODEL = 1024
SEQ = 2048
CHUNK = 64
HG_WIDTH = 512
HG_HEADS = 4
HG_DK = 128
GM_WIDTH = 512
GM_HEADS = 4
GM_DH = 128
GM_BLOCK = 128
D_FF = 2816
N_MOD = 9
IN_COLS = 4 * HG_WIDTH + 2 * GM_WIDTH
RMS_EPS = 1e-6
LN_EPS = 1e-5

FF_CHUNK = 256
N_FF_CHUNKS = D_FF // FF_CHUNK
FFN_ROWS = 512
MIX_ROWS = 512
SUB_ROWS = 256
ADA_COLS = 1024
HALF_SIZES = (32, 16, 8, 4, 2, 1)
N_LEVELS = len(HALF_SIZES)
FINE_SIZES = (4, 2, 1)
VMEM_LIMIT = 56 * 1024 * 1024

F32 = jnp.float32
BF16 = jnp.bfloat16


def _rms(x, eps):
    return x * lax.rsqrt(jnp.mean(x * x, axis=-1, keepdims=True) + eps)


def _resident(shape):
    return pl.BlockSpec(shape, lambda *_: (0,) * len(shape), pipeline_mode=pl.Buffered(1))


def _ada_kernel(c_ref, w_ref, b_ref, o_ref):
    c = c_ref[...]
    ca = c * jax.nn.sigmoid(c)
    o_ref[0] = jnp.dot(ca, w_ref[0], precision=lax.Precision.HIGHEST,
                       preferred_element_type=F32) + b_ref[0]


def _ada_call(c, w_ada, b_ada):
    depth, _, n_out = w_ada.shape
    batch = c.shape[0]
    return pl.pallas_call(
        _ada_kernel,
        grid=(depth, n_out // ADA_COLS),
        in_specs=[
            pl.BlockSpec((batch, D_MODEL), lambda l, j: (0, 0)),
            pl.BlockSpec((1, D_MODEL, ADA_COLS), lambda l, j: (l, 0, j)),
            pl.BlockSpec((1, 1, ADA_COLS), lambda l, j: (l, 0, j)),
        ],
        out_specs=pl.BlockSpec((1, batch, ADA_COLS), lambda l, j: (l, 0, j)),
        out_shape=jax.ShapeDtypeStruct((depth, batch, n_out), F32),
        compiler_params=pltpu.CompilerParams(
            dimension_semantics=("arbitrary", "arbitrary"), vmem_limit_bytes=VMEM_LIMIT),
        name="ada",
    )(c, w_ada, b_ada.reshape(depth, 1, n_out))


def _ffn_kernel(h_ref, mod_ref, ng_ref, w13_ref, w2_ref, fg_ref, o_ref, *, mod_base, final):
    h = h_ref[...]
    sh = mod_ref[0, mod_base:mod_base + 1, :]
    sc = mod_ref[0, mod_base + 1:mod_base + 2, :]
    gate = mod_ref[0, mod_base + 2:mod_base + 3, :]
    y = _rms(h, RMS_EPS) * ng_ref[...]
    yb = (y * (1.0 + sc) + sh).astype(BF16)
    acc = None
    for j in range(N_FF_CHUNKS):
        ab = jnp.dot(yb, w13_ref[j], preferred_element_type=F32)
        a = ab[:, :FF_CHUNK]
        b = ab[:, FF_CHUNK:]
        hm = (a * jax.nn.sigmoid(a) * b).astype(BF16)
        d = jnp.dot(hm, w2_ref[j], preferred_element_type=F32)
        acc = d if acc is None else acc + d
    res = h_ref[...] + 0.5 * gate * acc
    if final:
        res = _rms(res, RMS_EPS) * fg_ref[...]
    o_ref[...] = res


def _ffn_call(h2d, mod, ngain, w13c, w2c, fgain, *, mod_base, final):
    n_tok = h2d.shape[0]
    tiles_per_seq = SEQ // FFN_ROWS
    kern = functools.partial(_ffn_kernel, mod_base=mod_base, final=final)
    return pl.pallas_call(
        kern,
        grid=(n_tok // FFN_ROWS,),
        in_specs=[
            pl.BlockSpec((FFN_ROWS, D_MODEL), lambda i: (i, 0)),
            pl.BlockSpec((1, N_MOD, D_MODEL), lambda i: (i // tiles_per_seq, 0, 0)),
            _resident((1, D_MODEL)),
            _resident((N_FF_CHUNKS, D_MODEL, 2 * FF_CHUNK)),
            _resident((N_FF_CHUNKS, FF_CHUNK, D_MODEL)),
            _resident((1, D_MODEL)),
        ],
        out_specs=pl.BlockSpec((FFN_ROWS, D_MODEL), lambda i: (i, 0)),
        out_shape=jax.ShapeDtypeStruct(h2d.shape, F32),
        compiler_params=pltpu.CompilerParams(
            dimension_semantics=("arbitrary",), vmem_limit_bytes=VMEM_LIMIT),
        name="ffn_final" if final else "ffn",
    )(h2d, mod, ngain, w13c, w2c, fgain)


def _hgrn_constants():
    r = np.arange(CHUNK)
    sums = np.zeros((1 + len(FINE_SIZES), CHUNK, CHUNK), np.float32)
    pair = np.zeros((N_LEVELS + 1, CHUNK, CHUNK), np.float32)
    sums[0] = r[None, :] <= r[:, None]
    for i, m in enumerate(HALF_SIZES):
        blk = r // (2 * m)
        low = (r % (2 * m)) < m
        bnd = blk * 2 * m + m - 1
        pair[i] = (blk[:, None] == blk[None, :]) & (~low)[:, None] & low[None, :]
        if m in FINE_SIZES:
            fi = 1 + FINE_SIZES.index(m)
            for t in range(CHUNK):
                if low[t]:
                    sums[fi, t, t + 1:bnd[t] + 1] = 1.0
                else:
                    sums[fi, t, bnd[t] + 1:t + 1] = 1.0
    pair[N_LEVELS] = np.eye(CHUNK)
    sums = sums.reshape(-1, CHUNK)
    return np.concatenate([sums, sums], axis=1), pair


def _coarse_log_decay(b, m):
    parts = []
    for r0 in range(0, b.shape[0], 2 * m):
        bref = b[r0 + m - 1:r0 + m, :]
        parts.append(bref - b[r0:r0 + m])
        parts.append(b[r0 + m:r0 + 2 * m] - bref)
    return jnp.concatenate(parts, axis=0)


def _select_rows(m, upper_val, lower_val):
    rows = upper_val.shape[0]
    if m >= 8:
        parts = []
        for r0 in range(0, rows, m):
            src = upper_val if (r0 // m) % 2 == 1 else lower_val
            parts.append(src[r0:r0 + m])
        return jnp.concatenate(parts, axis=0)
    row = lax.broadcasted_iota(jnp.int32, upper_val.shape, 0)
    return jnp.where((row & m) != 0, upper_val, lower_val)


def _mixer_kernel(h_ref, mod_ref, ng_ref, win_ref, wout_ref, lbl_ref, gn_ref, lng_ref, og_ref,
                  wsp_ref, bsp_ref, sums_ref, pair_ref, o_ref, st_ref, proj_ref, cat_ref,
                  *, layer):
    @pl.when(pl.program_id(1) == 0)
    def _():
        st_ref[...] = jnp.zeros_like(st_ref)

    sh = mod_ref[0, 3:4, :]
    sc = mod_ref[0, 4:5, :]
    gate = mod_ref[0, 5:6, :]

    def project(r0):
        h = h_ref[0, r0:r0 + SUB_ROWS, :]
        y = _rms(h, RMS_EPS) * ng_ref[...]
        y = (y * (1.0 + sc) + sh).astype(BF16)
        proj_ref[r0:r0 + SUB_ROWS, :] = jnp.dot(y, win_ref[...], preferred_element_type=F32)

    def out_project(r0):
        mix = jnp.dot(cat_ref[r0:r0 + SUB_ROWS, :], wout_ref[...], preferred_element_type=F32)
        o_ref[0, r0:r0 + SUB_ROWS, :] = h_ref[0, r0:r0 + SUB_ROWS, :] + gate * mix

    lg = lbl_ref[...]
    ex = jnp.exp(lg - jnp.max(lg, axis=0, keepdims=True))
    prob = ex / jnp.sum(ex, axis=0, keepdims=True)
    lb = jnp.zeros((1, HG_WIDTH), F32)
    for i in range(1, layer + 1):
        lb = lb + prob[i:i + 1, :]

    r2 = lax.broadcasted_iota(jnp.int32, (GM_BLOCK, GM_BLOCK), 0)
    c2 = lax.broadcasted_iota(jnp.int32, (GM_BLOCK, GM_BLOCK), 1)
    causal = (r2 // CHUNK) >= (c2 // CHUNK)
    w_sp = [jnp.where(causal, wsp_ref[hd], 0.0).astype(BF16) for hd in range(GM_HEADS)]

    def gm_block(r0):
        rows = slice(r0, r0 + GM_BLOCK)
        for hd in range(GM_HEADS):
            cu = 4 * HG_WIDTH + hd * GM_DH
            cv = cu + GM_WIDTH
            u = jax.nn.gelu(proj_ref[rows, cu:cu + GM_DH])
            v = jax.nn.gelu(proj_ref[rows, cv:cv + GM_DH])
            vc = v - jnp.mean(v, axis=-1, keepdims=True)
            var = jnp.mean(vc * vc, axis=-1, keepdims=True)
            vn = vc * lax.rsqrt(var + LN_EPS) * lng_ref[:, hd * GM_DH:(hd + 1) * GM_DH]
            mixed = jnp.dot(w_sp[hd], vn.astype(BF16), preferred_element_type=F32) + bsp_ref[hd]
            z = _rms(u * mixed, RMS_EPS) * og_ref[:, hd * GM_DH:(hd + 1) * GM_DH]
            cat_ref[rows, HG_WIDTH + hd * GM_DH:HG_WIDTH + (hd + 1) * GM_DH] = z.astype(BF16)

    def hg_chunk(r0):
        rows = slice(r0, r0 + CHUNK)
        q = proj_ref[rows, 0:HG_WIDTH]
        f_raw = proj_ref[rows, HG_WIDTH:2 * HG_WIDTH]
        f = lb + (1.0 - lb) * jax.nn.sigmoid(f_raw)
        log_f = jnp.log(f)
        k = 1.0 - f
        hi = log_f.astype(BF16)
        mid = (log_f - hi.astype(F32)).astype(BF16)
        sums = jnp.dot(sums_ref[...], jnp.concatenate([hi, mid], axis=0),
                       preferred_element_type=F32)
        b = sums[0:CHUNK]
        decay_in = jnp.exp(b)
        decay_out = jnp.exp(b[CHUNK - 1:CHUNK, :] - b)
        lvl = []
        for m in HALF_SIZES:
            if m in FINE_SIZES:
                fi = 1 + FINE_SIZES.index(m)
                log_decay = sums[fi * CHUNK:(fi + 1) * CHUNK]
            else:
                log_decay = _coarse_log_decay(b, m)
            lvl.append((_select_rows(m, q, k) * jnp.exp(log_decay)).astype(BF16))
        qb = q.astype(BF16)
        kb = k.astype(BF16)
        q_in = (q * decay_in).astype(BF16)
        k_out = (k * decay_out).astype(BF16)
        for hd in range(HG_HEADS):
            cs = slice(hd * HG_DK, (hd + 1) * HG_DK)
            dn = (((1,), (1,)), ((), ()))
            att = lax.dot_general(qb[:, cs], kb[:, cs], dn, preferred_element_type=F32) * pair_ref[N_LEVELS]
            for i in range(N_LEVELS):
                mh = lvl[i][:, cs]
                att = att + lax.dot_general(mh, mh, dn, preferred_element_type=F32) * pair_ref[i]
            vb = proj_ref[rows, 2 * HG_WIDTH + hd * HG_DK:2 * HG_WIDTH + (hd + 1) * HG_DK].astype(BF16)
            st = st_ref[hd]
            o = (jnp.dot(att.astype(BF16), vb, preferred_element_type=F32)
                 + lax.dot_general(q_in[:, cs], st.astype(BF16), dn, preferred_element_type=F32))
            st_ref[hd] = (st * decay_in[CHUNK - 1:CHUNK, cs]
                          + lax.dot_general(vb, k_out[:, cs], (((0,), (0,)), ((), ())),
                                            preferred_element_type=F32))
            g = proj_ref[rows, 3 * HG_WIDTH + hd * HG_DK:3 * HG_WIDTH + (hd + 1) * HG_DK]
            o = _rms(o, RMS_EPS) * gn_ref[:, cs] * (g * jax.nn.sigmoid(g))
            cat_ref[rows, cs] = o.astype(BF16)

    sub_tiles = range(0, MIX_ROWS, SUB_ROWS)
    for r0 in sub_tiles:
        project(r0)
    for r0 in sub_tiles:
        for rb in range(r0, r0 + SUB_ROWS, GM_BLOCK):
            gm_block(rb)
        for rc in range(r0, r0 + SUB_ROWS, CHUNK):
            hg_chunk(rc)
        out_project(r0)


def _mixer_call(h, mod, ngain, w_in, w_out, lb_logits, gnorm, ln_gain, out_gain, w_sp, b_sp,
                sums_c, pair_c, *, layer):
    batch = h.shape[0]
    depth = lb_logits.shape[0]
    kern = functools.partial(_mixer_kernel, layer=layer)
    return pl.pallas_call(
        kern,
        grid=(batch, SEQ // MIX_ROWS),
        in_specs=[
            pl.BlockSpec((1, MIX_ROWS, D_MODEL), lambda b, j: (b, j, 0)),
            pl.BlockSpec((1, N_MOD, D_MODEL), lambda b, j: (b, 0, 0)),
            _resident((1, D_MODEL)),
            _resident((D_MODEL, IN_COLS)),
            _resident((D_MODEL, D_MODEL)),
            _resident((depth, HG_WIDTH)),
            _resident((1, HG_WIDTH)),
            _resident((1, GM_WIDTH)),
            _resident((1, GM_WIDTH)),
            _resident((GM_HEADS, GM_BLOCK, GM_BLOCK)),
            _resident((GM_HEADS, GM_BLOCK, 1)),
            _resident(sums_c.shape),
            _resident(pair_c.shape),
        ],
        out_specs=pl.BlockSpec((1, MIX_ROWS, D_MODEL), lambda b, j: (b, j, 0)),
        out_shape=jax.ShapeDtypeStruct(h.shape, F32),
        scratch_shapes=[pltpu.VMEM((HG_HEADS, HG_DK, HG_DK), F32),
                        pltpu.VMEM((MIX_ROWS, IN_COLS), F32),
                        pltpu.VMEM((MIX_ROWS, D_MODEL), BF16)],
        compiler_params=pltpu.CompilerParams(
            dimension_semantics=("arbitrary", "arbitrary"), vmem_limit_bytes=VMEM_LIMIT),
        name="mixer",
    )(h, mod, ngain, w_in, w_out, lb_logits, gnorm, ln_gain, out_gain, w_sp, b_sp, sums_c, pair_c)


def _ffn_weights(w13, w2):
    w1 = w13[:, :D_FF].reshape(D_MODEL, N_FF_CHUNKS, FF_CHUNK)
    w3 = w13[:, D_FF:].reshape(D_MODEL, N_FF_CHUNKS, FF_CHUNK)
    w13c = jnp.concatenate([w1, w3], axis=2).transpose(1, 0, 2).astype(BF16)
    w2c = w2.reshape(N_FF_CHUNKS, FF_CHUNK, D_MODEL).astype(BF16)
    return w13c, w2c


def kernel(x, c, w_ada, b_ada, norm_gain, ffn1_w13, ffn1_w2, w_in, hg_lb_logits, hg_gnorm,
           gm_ln_gain, gm_w_spatial, gm_b_spatial, gm_out_gain, w_out, ffn2_w13, ffn2_w2,
           final_gain):
    batch, seq, _ = x.shape
    depth = w_ada.shape[0]
    sums_np, pair_np = _hgrn_constants()
    sums_c = jnp.asarray(sums_np, BF16)
    pair_c = jnp.asarray(pair_np, F32)
    fgain = final_gain.reshape(1, D_MODEL)

    mod_all = _ada_call(c, w_ada, b_ada).reshape(depth, batch, N_MOD, D_MODEL)
    h = x
    for l in range(depth):
        mod = mod_all[l]
        w13c, w2c = _ffn_weights(ffn1_w13[l], ffn1_w2[l])
        h = _ffn_call(h.reshape(batch * seq, D_MODEL), mod, norm_gain[l, 0:1], w13c, w2c, fgain,
                      mod_base=0, final=False).reshape(batch, seq, D_MODEL)
        h = _mixer_call(h, mod, norm_gain[l, 1:2], w_in[l].astype(BF16), w_out[l].astype(BF16),
                        hg_lb_logits, hg_gnorm[l:l + 1], gm_ln_gain[l:l + 1], gm_out_gain[l:l + 1],
                        gm_w_spatial[l], gm_b_spatial[l].reshape(GM_HEADS, GM_BLOCK, 1),
                        sums_c, pair_c, layer=l)
        w13c, w2c = _ffn_weights(ffn2_w13[l], ffn2_w2[l])
        h = _ffn_call(h.reshape(batch * seq, D_MODEL), mod, norm_gain[l, 2:3], w13c, w2c, fgain,
                      mod_base=6, final=(l == depth - 1)).reshape(batch, seq, D_MODEL)
    return h
```

```python
import functools

import numpy as np
import jax
import jax.numpy as jnp
from jax import lax
from jax.experimental import pallas as pl
from jax.experimental.pallas import tpu as pltpu

D_MODEL = 1024
SEQ = 2048
CHUNK = 64
HG_WIDTH = 512
HG_HEADS = 4
HG_DK = 128
GM_WIDTH = 512
GM_HEADS = 4
GM_DH = 128
GM_BLOCK = 128
D_FF = 2816
N_MOD = 9
IN_COLS = 4 * HG_WIDTH + 2 * GM_WIDTH
RMS_EPS = 1e-6
LN_EPS = 1e-5

FF_CHUNK = 256
N_FF_CHUNKS = D_FF // FF_CHUNK
FFN_ROWS = 512
MIX_ROWS = 512
SUB_ROWS = 256
ADA_COLS = 1024
HALF_SIZES = (32, 16, 8, 4, 2, 1)
N_LEVELS = len(HALF_SIZES)
FINE_SIZES = (4, 2, 1)
VMEM_LIMIT = 56 * 1024 * 1024
LOG2_E = 1.4426950408889634

F32 = jnp.float32
BF16 = jnp.bfloat16


def _rms(x, eps):
    return x * lax.rsqrt(jnp.mean(x * x, axis=-1, keepdims=True) + eps)


def _resident(shape):
    return pl.BlockSpec(shape, lambda *_: (0,) * len(shape), pipeline_mode=pl.Buffered(1))


def _ada_kernel(c_ref, w_ref, b_ref, o_ref):
    c = c_ref[...]
    ca = c * jax.nn.sigmoid(c)
    o_ref[0] = jnp.dot(ca, w_ref[0], precision=lax.Precision.HIGHEST,
                       preferred_element_type=F32) + b_ref[0]


def _ada_call(c, w_ada, b_ada):
    depth, _, n_out = w_ada.shape
    batch = c.shape[0]
    return pl.pallas_call(
        _ada_kernel,
        grid=(depth, n_out // ADA_COLS),
        in_specs=[
            pl.BlockSpec((batch, D_MODEL), lambda l, j: (0, 0)),
            pl.BlockSpec((1, D_MODEL, ADA_COLS), lambda l, j: (l, 0, j)),
            pl.BlockSpec((1, 1, ADA_COLS), lambda l, j: (l, 0, j)),
        ],
        out_specs=pl.BlockSpec((1, batch, ADA_COLS), lambda l, j: (l, 0, j)),
        out_shape=jax.ShapeDtypeStruct((depth, batch, n_out), F32),
        compiler_params=pltpu.CompilerParams(
            dimension_semantics=("arbitrary", "arbitrary"), vmem_limit_bytes=VMEM_LIMIT),
        name="ada",
    )(c, w_ada, b_ada.reshape(depth, 1, n_out))


def _ffn_kernel(h_ref, mod_ref, ng_ref, w13_ref, w2_ref, fg_ref, o_ref, *, mod_base, final):
    h = h_ref[...]
    sh = mod_ref[0, mod_base:mod_base + 1, :]
    sc = mod_ref[0, mod_base + 1:mod_base + 2, :]
    gate = mod_ref[0, mod_base + 2:mod_base + 3, :]
    y = _rms(h, RMS_EPS) * ng_ref[...]
    yb = (y * (1.0 + sc) + sh).astype(BF16)
    acc = None
    for j in range(N_FF_CHUNKS):
        ab = jnp.dot(yb, w13_ref[j], preferred_element_type=F32)
        a = ab[:, :FF_CHUNK]
        b = ab[:, FF_CHUNK:]
        hm = (a * jax.nn.sigmoid(a) * b).astype(BF16)
        d = jnp.dot(hm, w2_ref[j], preferred_element_type=F32)
        acc = d if acc is None else acc + d
    res = h_ref[...] + 0.5 * gate * acc
    if final:
        res = _rms(res, RMS_EPS) * fg_ref[...]
    o_ref[...] = res


def _ffn_call(h2d, mod, ngain, w13c, w2c, fgain, *, mod_base, final):
    n_tok = h2d.shape[0]
    tiles_per_seq = SEQ // FFN_ROWS
    kern = functools.partial(_ffn_kernel, mod_base=mod_base, final=final)
    return pl.pallas_call(
        kern,
        grid=(n_tok // FFN_ROWS,),
        in_specs=[
            pl.BlockSpec((FFN_ROWS, D_MODEL), lambda i: (i, 0)),
            pl.BlockSpec((1, N_MOD, D_MODEL), lambda i: (i // tiles_per_seq, 0, 0)),
            _resident((1, D_MODEL)),
            _resident((N_FF_CHUNKS, D_MODEL, 2 * FF_CHUNK)),
            _resident((N_FF_CHUNKS, FF_CHUNK, D_MODEL)),
            _resident((1, D_MODEL)),
        ],
        out_specs=pl.BlockSpec((FFN_ROWS, D_MODEL), lambda i: (i, 0)),
        out_shape=jax.ShapeDtypeStruct(h2d.shape, F32),
        compiler_params=pltpu.CompilerParams(
            dimension_semantics=("arbitrary",), vmem_limit_bytes=VMEM_LIMIT),
        name="ffn_final" if final else "ffn",
    )(h2d, mod, ngain, w13c, w2c, fgain)


def _hgrn_constants():
    r = np.arange(CHUNK)
    sums = np.zeros((1 + len(FINE_SIZES), CHUNK, CHUNK), np.float32)
    pair = np.zeros((N_LEVELS + 1, CHUNK, CHUNK), np.float32)
    sums[0] = r[None, :] <= r[:, None]
    for i, m in enumerate(HALF_SIZES):
        blk = r // (2 * m)
        low = (r % (2 * m)) < m
        bnd = blk * 2 * m + m - 1
        pair[i] = (blk[:, None] == blk[None, :]) & (~low)[:, None] & low[None, :]
        if m in FINE_SIZES:
            fi = 1 + FINE_SIZES.index(m)
            for t in range(CHUNK):
                if low[t]:
                    sums[fi, t, t + 1:bnd[t] + 1] = 1.0
                else:
                    sums[fi, t, bnd[t] + 1:t + 1] = 1.0
    pair[N_LEVELS] = np.eye(CHUNK)
    sums = sums.reshape(-1, CHUNK)
    return np.concatenate([sums, sums], axis=1), pair


def _coarse_log_decay(b, m):
    parts = []
    for r0 in range(0, b.shape[0], 2 * m):
        bref = b[r0 + m - 1:r0 + m, :]
        parts.append(bref - b[r0:r0 + m])
        parts.append(b[r0 + m:r0 + 2 * m] - bref)
    return jnp.concatenate(parts, axis=0)


def _select_rows(m, upper_val, lower_val):
    rows = upper_val.shape[0]
    if m >= 8:
        parts = []
        for r0 in range(0, rows, m):
            src = upper_val if (r0 // m) % 2 == 1 else lower_val
            parts.append(src[r0:r0 + m])
        return jnp.concatenate(parts, axis=0)
    row = lax.broadcasted_iota(jnp.int32, upper_val.shape, 0)
    return jnp.where((row & m) != 0, upper_val, lower_val)


def _mixer_kernel(h_ref, mod_ref, ng_ref, win_ref, wout_ref, lbl_ref, gn_ref, lng_ref, og_ref,
                  wsp_ref, bsp_ref, sums_ref, pair_ref, o_ref, st_ref, proj_ref, *, layer):
    @pl.when(pl.program_id(1) == 0)
    def _():
        st_ref[...] = jnp.zeros_like(st_ref)

    sh = mod_ref[0, 3:4, :]
    sc = mod_ref[0, 4:5, :]
    gate = mod_ref[0, 5:6, :]

    def normalize(r0):
        y = _rms(h_ref[0, r0:r0 + SUB_ROWS, :], RMS_EPS) * ng_ref[...]
        return (y * (1.0 + sc) + sh).astype(BF16)

    def project(r0, y, c0, c1):
        proj_ref[r0:r0 + SUB_ROWS, c0:c1] = jnp.dot(y, win_ref[:, c0:c1], preferred_element_type=F32)

    def out_project(r0, cat, c0, c1):
        mix = jnp.dot(cat, wout_ref[:, c0:c1], preferred_element_type=F32)
        o_ref[0, r0:r0 + SUB_ROWS, c0:c1] = h_ref[0, r0:r0 + SUB_ROWS, c0:c1] + gate[:, c0:c1] * mix

    lg = lbl_ref[...]
    ex = jnp.exp(lg - jnp.max(lg, axis=0, keepdims=True))
    prob = ex / jnp.sum(ex, axis=0, keepdims=True)
    lb = jnp.zeros((1, HG_WIDTH), F32)
    for i in range(1, layer + 1):
        lb = lb + prob[i:i + 1, :]

    r2 = lax.broadcasted_iota(jnp.int32, (GM_BLOCK, GM_BLOCK), 0)
    c2 = lax.broadcasted_iota(jnp.int32, (GM_BLOCK, GM_BLOCK), 1)
    causal = (r2 // CHUNK) >= (c2 // CHUNK)
    w_sp = [jnp.where(causal, wsp_ref[hd], 0.0).astype(BF16) for hd in range(GM_HEADS)]

    def gm_mix(r0):
        rows = slice(r0, r0 + GM_BLOCK)
        res = []
        for hd in range(GM_HEADS):
            cu = 4 * HG_WIDTH + hd * GM_DH
            cv = cu + GM_WIDTH
            u = jax.nn.gelu(proj_ref[rows, cu:cu + GM_DH])
            v = jax.nn.gelu(proj_ref[rows, cv:cv + GM_DH])
            vc = v - jnp.mean(v, axis=-1, keepdims=True)
            var = jnp.mean(vc * vc, axis=-1, keepdims=True)
            vn = vc * lax.rsqrt(var + LN_EPS) * lng_ref[:, hd * GM_DH:(hd + 1) * GM_DH]
            mixed = jnp.dot(w_sp[hd], vn.astype(BF16), preferred_element_type=F32) + bsp_ref[hd]
            res.append((u, mixed))
        return res

    def gm_gate(res):
        outs = []
        for hd, (u, mixed) in enumerate(res):
            z = _rms(u * mixed, RMS_EPS) * og_ref[:, hd * GM_DH:(hd + 1) * GM_DH]
            outs.append(z.astype(BF16))
        return jnp.concatenate(outs, axis=1)

    def hg_gates(r0):
        rows = slice(r0, r0 + CHUNK)
        q = proj_ref[rows, 0:HG_WIDTH]
        f_raw = proj_ref[rows, HG_WIDTH:2 * HG_WIDTH]
        f = lb + (1.0 - lb) * jax.nn.sigmoid(f_raw)
        log_f = jnp.log(f) * LOG2_E
        k = 1.0 - f
        hi = log_f.astype(BF16)
        mid = (log_f - hi.astype(F32)).astype(BF16)
        sums = jnp.dot(sums_ref[...], jnp.concatenate([hi, mid], axis=0),
                       preferred_element_type=F32)
        return q, k, sums

    def hg_scores(r0, gates):
        rows = slice(r0, r0 + CHUNK)
        q, k, sums = gates
        b = sums[0:CHUNK]
        decay_in = jnp.exp2(b)
        decay_out = jnp.exp2(b[CHUNK - 1:CHUNK, :] - b)
        lvl = []
        for m in HALF_SIZES:
            if m in FINE_SIZES:
                fi = 1 + FINE_SIZES.index(m)
                log_decay = sums[fi * CHUNK:(fi + 1) * CHUNK]
            else:
                log_decay = _coarse_log_decay(b, m)
            lvl.append((_select_rows(m, q, k) * jnp.exp2(log_decay)).astype(BF16))
        qb = q.astype(BF16)
        kb = k.astype(BF16)
        q_in = (q * decay_in).astype(BF16)
        k_out = (k * decay_out).astype(BF16)
        dn = (((1,), (1,)), ((), ()))
        per_head = []
        for hd in range(HG_HEADS):
            cs = slice(hd * HG_DK, (hd + 1) * HG_DK)
            att = lax.dot_general(qb[:, cs], kb[:, cs], dn, preferred_element_type=F32) * pair_ref[N_LEVELS]
            for i in range(N_LEVELS):
                mh = lvl[i][:, cs]
                att = att + lax.dot_general(mh, mh, dn, preferred_element_type=F32) * pair_ref[i]
            vb = proj_ref[rows, 2 * HG_WIDTH + hd * HG_DK:2 * HG_WIDTH + (hd + 1) * HG_DK].astype(BF16)
            inc = lax.dot_general(vb, k_out[:, cs], (((0,), (0,)), ((), ())),
                                  preferred_element_type=F32)
            per_head.append((att.astype(BF16), vb, q_in[:, cs], inc, decay_in[CHUNK - 1:CHUNK, cs]))
        return per_head

    def hg_output(r0, per_head):
        rows = slice(r0, r0 + CHUNK)
        dn = (((1,), (1,)), ((), ()))
        outs = []
        for hd, (att, vb, q_in, inc, decay_last) in enumerate(per_head):
            cs = slice(hd * HG_DK, (hd + 1) * HG_DK)
            st = st_ref[hd]
            o = (jnp.dot(att, vb, preferred_element_type=F32)
                 + lax.dot_general(q_in, st.astype(BF16), dn, preferred_element_type=F32))
            st_ref[hd] = st * decay_last + inc
            g = proj_ref[rows, 3 * HG_WIDTH + hd * HG_DK:3 * HG_WIDTH + (hd + 1) * HG_DK]
            o = _rms(o, RMS_EPS) * gn_ref[:, cs] * (g * jax.nn.sigmoid(g))
            outs.append(o.astype(BF16))
        return jnp.concatenate(outs, axis=1)

    n_sub = MIX_ROWS // SUB_ROWS
    n_chunks = MIX_ROWS // CHUNK
    chunks_per_sub = SUB_ROWS // CHUNK
    chunks_per_blk = GM_BLOCK // CHUNK
    in_piece = IN_COLS // chunks_per_sub
    out_piece = D_MODEL // chunks_per_sub
    lag_out = 2

    y = [normalize(s * SUB_ROWS) for s in range(n_sub)]
    project(0, y[0], 0, IN_COLS)
    gates, scores, gm_state, hg_out, gm_out, cat = {}, {}, {}, {}, {}, {}
    for slot in range(n_chunks + lag_out + chunks_per_sub):
        s_next = slot // chunks_per_sub + 1
        if s_next < n_sub:
            p = slot % chunks_per_sub
            project(s_next * SUB_ROWS, y[s_next], p * in_piece, (p + 1) * in_piece)
        done = slot - lag_out - chunks_per_sub
        if done >= 0 and done // chunks_per_sub < n_sub:
            s_done, p = divmod(done, chunks_per_sub)
            if p == 0:
                c0 = s_done * chunks_per_sub
                hg_part = jnp.concatenate([hg_out.pop(c0 + i) for i in range(chunks_per_sub)], axis=0)
                gm_part = jnp.concatenate(
                    [gm_out.pop(c0 + i) for i in range(0, chunks_per_sub, chunks_per_blk)], axis=0)
                cat[s_done] = jnp.concatenate([hg_part, gm_part], axis=1)
            out_project(s_done * SUB_ROWS, cat[s_done], p * out_piece, (p + 1) * out_piece)
        if slot < n_chunks:
            gates[slot] = hg_gates(slot * CHUNK)
            if slot % chunks_per_blk == 0:
                gm_state[slot] = gm_mix(slot * CHUNK)
            else:
                gm_out[slot - 1] = gm_gate(gm_state.pop(slot - 1))
        if 0 <= slot - 1 < n_chunks:
            scores[slot - 1] = hg_scores((slot - 1) * CHUNK, gates.pop(slot - 1))
        if 0 <= slot - lag_out < n_chunks:
            hg_out[slot - lag_out] = hg_output((slot - lag_out) * CHUNK, scores.pop(slot - lag_out))


def _mixer_call(h, mod, ngain, w_in, w_out, lb_logits, gnorm, ln_gain, out_gain, w_sp, b_sp,
                sums_c, pair_c, *, layer):
    batch = h.shape[0]
    depth = lb_logits.shape[0]
    kern = functools.partial(_mixer_kernel, layer=layer)
    return pl.pallas_call(
        kern,
        grid=(batch, SEQ // MIX_ROWS),
        in_specs=[
            pl.BlockSpec((1, MIX_ROWS, D_MODEL), lambda b, j: (b, j, 0)),
            pl.BlockSpec((1, N_MOD, D_MODEL), lambda b, j: (b, 0, 0)),
            _resident((1, D_MODEL)),
            _resident((D_MODEL, IN_COLS)),
            _resident((D_MODEL, D_MODEL)),
            _resident((depth, HG_WIDTH)),
            _resident((1, HG_WIDTH)),
            _resident((1, GM_WIDTH)),
            _resident((1, GM_WIDTH)),
            _resident((GM_HEADS, GM_BLOCK, GM_BLOCK)),
            _resident((GM_HEADS, GM_BLOCK, 1)),
            _resident(sums_c.shape),
            _resident(pair_c.shape),
        ],
        out_specs=pl.BlockSpec((1, MIX_ROWS, D_MODEL), lambda b, j: (b, j, 0)),
        out_shape=jax.ShapeDtypeStruct(h.shape, F32),
        scratch_shapes=[pltpu.VMEM((HG_HEADS, HG_DK, HG_DK), F32),
                        pltpu.VMEM((MIX_ROWS, IN_COLS), F32)],
        compiler_params=pltpu.CompilerParams(
            dimension_semantics=("arbitrary", "arbitrary"), vmem_limit_bytes=VMEM_LIMIT),
        name="mixer",
    )(h, mod, ngain, w_in, w_out, lb_logits, gnorm, ln_gain, out_gain, w_sp, b_sp, sums_c, pair_c)


def _ffn_weights(w13, w2):
    w1 = w13[:, :D_FF].reshape(D_MODEL, N_FF_CHUNKS, FF_CHUNK)
    w3 = w13[:, D_FF:].reshape(D_MODEL, N_FF_CHUNKS, FF_CHUNK)
    w13c = jnp.concatenate([w1, w3], axis=2).transpose(1, 0, 2).astype(BF16)
    w2c = w2.reshape(N_FF_CHUNKS, FF_CHUNK, D_MODEL).astype(BF16)
    return w13c, w2c


def kernel(x, c, w_ada, b_ada, norm_gain, ffn1_w13, ffn1_w2, w_in, hg_lb_logits, hg_gnorm,
           gm_ln_gain, gm_w_spatial, gm_b_spatial, gm_out_gain, w_out, ffn2_w13, ffn2_w2,
           final_gain):
    batch, seq, _ = x.shape
    depth = w_ada.shape[0]
    sums_np, pair_np = _hgrn_constants()
    sums_c = jnp.asarray(sums_np, BF16)
    pair_c = jnp.asarray(pair_np, F32)
    fgain = final_gain.reshape(1, D_MODEL)

    mod_all = _ada_call(c, w_ada, b_ada).reshape(depth, batch, N_MOD, D_MODEL)
    h = x
    for l in range(depth):
        mod = mod_all[l]
        w13c, w2c = _ffn_weights(ffn1_w13[l], ffn1_w2[l])
        h = _ffn_call(h.reshape(batch * seq, D_MODEL), mod, norm_gain[l, 0:1], w13c, w2c, fgain,
                      mod_base=0, final=False).reshape(batch, seq, D_MODEL)
        h = _mixer_call(h, mod, norm_gain[l, 1:2], w_in[l].astype(BF16), w_out[l].astype(BF16),
                        hg_lb_logits, hg_gnorm[l:l + 1], gm_ln_gain[l:l + 1], gm_out_gain[l:l + 1],
                        gm_w_spatial[l], gm_b_spatial[l].reshape(GM_HEADS, GM_BLOCK, 1),
                        sums_c, pair_c, layer=l)
        w13c, w2c = _ffn_weights(ffn2_w13[l], ffn2_w2[l])
        h = _ffn_call(h.reshape(batch * seq, D_MODEL), mod, norm_gain[l, 2:3], w13c, w2c, fgain,
                      mod_base=6, final=(l == depth - 1)).reshape(batch, seq, D_MODEL)
    return h
```

```python
import functools

import numpy as np
import jax
import jax.numpy as jnp
from jax import lax
from jax.experimental import pallas as pl
from jax.experimental.pallas import tpu as pltpu

D_MODEL = 1024
SEQ = 2048
CHUNK = 64
HG_WIDTH = 512
HG_HEADS = 4
HG_DK = 128
GM_WIDTH = 512
GM_HEADS = 4
GM_DH = 128
GM_BLOCK = 128
D_FF = 2816
N_MOD = 9
IN_COLS = 4 * HG_WIDTH + 2 * GM_WIDTH
RMS_EPS = 1e-6
LN_EPS = 1e-5

FF_CHUNK = 256
N_FF_CHUNKS = D_FF // FF_CHUNK
FFN_ROWS = 512
MIX_ROWS = 512
SUB_ROWS = 256
ADA_COLS = 1024
HALF_SIZES = (32, 16, 8, 4, 2, 1)
N_LEVELS = len(HALF_SIZES)
FINE_SIZES = (4, 2, 1)
VMEM_LIMIT = 56 * 1024 * 1024
LOG2_E = 1.4426950408889634

F32 = jnp.float32
BF16 = jnp.bfloat16


def _rms(x, eps):
    return x * lax.rsqrt(jnp.mean(x * x, axis=-1, keepdims=True) + eps)


def _resident(shape):
    return pl.BlockSpec(shape, lambda *_: (0,) * len(shape), pipeline_mode=pl.Buffered(1))


def _ada_kernel(c_ref, w_ref, b_ref, o_ref):
    c = c_ref[...]
    ca = c * jax.nn.sigmoid(c)
    batch = ca.shape[0]
    ca_hi = ca.astype(BF16)
    ca_lo = (ca - ca_hi.astype(F32)).astype(BF16)
    w = w_ref[0]
    w_hi = w.astype(BF16)
    w_lo = (w - w_hi.astype(F32)).astype(BF16)
    p = jnp.dot(jnp.concatenate([ca_hi, ca_lo], axis=0), w_hi, preferred_element_type=F32)
    q = jnp.dot(ca_hi, w_lo, preferred_element_type=F32)
    o_ref[0] = p[:batch] + p[batch:] + q + b_ref[0]


def _ada_call(c, w_ada, b_ada):
    depth, _, n_out = w_ada.shape
    batch = c.shape[0]
    return pl.pallas_call(
        _ada_kernel,
        grid=(depth, n_out // ADA_COLS),
        in_specs=[
            pl.BlockSpec((batch, D_MODEL), lambda l, j: (0, 0)),
            pl.BlockSpec((1, D_MODEL, ADA_COLS), lambda l, j: (l, 0, j)),
            pl.BlockSpec((1, 1, ADA_COLS), lambda l, j: (l, 0, j)),
        ],
        out_specs=pl.BlockSpec((1, batch, ADA_COLS), lambda l, j: (l, 0, j)),
        out_shape=jax.ShapeDtypeStruct((depth, batch, n_out), F32),
        compiler_params=pltpu.CompilerParams(
            dimension_semantics=("arbitrary", "arbitrary"), vmem_limit_bytes=VMEM_LIMIT),
        name="ada",
    )(c, w_ada, b_ada.reshape(depth, 1, n_out))


def _ffn_kernel(h_ref, mod_ref, ng_ref, w13_ref, w2_ref, fg_ref, o_ref, *, mod_base, final):
    h = h_ref[...]
    sh = mod_ref[0, mod_base:mod_base + 1, :]
    sc = mod_ref[0, mod_base + 1:mod_base + 2, :]
    gate = mod_ref[0, mod_base + 2:mod_base + 3, :]
    y = _rms(h, RMS_EPS) * ng_ref[...]
    yb = (y * (1.0 + sc) + sh).astype(BF16)
    acc = None
    for j in range(N_FF_CHUNKS):
        c0 = j * FF_CHUNK
        a = jnp.dot(yb, w13_ref[:, c0:c0 + FF_CHUNK], preferred_element_type=F32)
        b = jnp.dot(yb, w13_ref[:, D_FF + c0:D_FF + c0 + FF_CHUNK], preferred_element_type=F32)
        hm = (a * jax.nn.sigmoid(a) * b).astype(BF16)
        d = jnp.dot(hm, w2_ref[c0:c0 + FF_CHUNK, :], preferred_element_type=F32)
        acc = d if acc is None else acc + d
    res = h_ref[...] + 0.5 * gate * acc
    if final:
        res = _rms(res, RMS_EPS) * fg_ref[...]
    o_ref[...] = res


def _ffn_call(h2d, mod, ngain, w13c, w2c, fgain, *, mod_base, final):
    n_tok = h2d.shape[0]
    tiles_per_seq = SEQ // FFN_ROWS
    kern = functools.partial(_ffn_kernel, mod_base=mod_base, final=final)
    return pl.pallas_call(
        kern,
        grid=(n_tok // FFN_ROWS,),
        in_specs=[
            pl.BlockSpec((FFN_ROWS, D_MODEL), lambda i: (i, 0)),
            pl.BlockSpec((1, N_MOD, D_MODEL), lambda i: (i // tiles_per_seq, 0, 0)),
            _resident((1, D_MODEL)),
            _resident((D_MODEL, 2 * D_FF)),
            _resident((D_FF, D_MODEL)),
            _resident((1, D_MODEL)),
        ],
        out_specs=pl.BlockSpec((FFN_ROWS, D_MODEL), lambda i: (i, 0)),
        out_shape=jax.ShapeDtypeStruct(h2d.shape, F32),
        compiler_params=pltpu.CompilerParams(
            dimension_semantics=("arbitrary",), vmem_limit_bytes=VMEM_LIMIT),
        name="ffn_final" if final else "ffn",
    )(h2d, mod, ngain, w13c, w2c, fgain)


def _hgrn_constants():
    r = np.arange(CHUNK)
    sums = np.zeros((1 + len(FINE_SIZES), CHUNK, CHUNK), np.float32)
    pair = np.zeros((N_LEVELS + 1, CHUNK, CHUNK), np.float32)
    sums[0] = r[None, :] <= r[:, None]
    for i, m in enumerate(HALF_SIZES):
        blk = r // (2 * m)
        low = (r % (2 * m)) < m
        bnd = blk * 2 * m + m - 1
        pair[i] = (blk[:, None] == blk[None, :]) & (~low)[:, None] & low[None, :]
        if m in FINE_SIZES:
            fi = 1 + FINE_SIZES.index(m)
            for t in range(CHUNK):
                if low[t]:
                    sums[fi, t, t + 1:bnd[t] + 1] = 1.0
                else:
                    sums[fi, t, bnd[t] + 1:t + 1] = 1.0
    pair[N_LEVELS] = np.eye(CHUNK)
    sums = sums.reshape(-1, CHUNK)
    return np.concatenate([sums, sums], axis=1), pair


def _coarse_log_decay(b, m):
    parts = []
    for r0 in range(0, b.shape[0], 2 * m):
        bref = b[r0 + m - 1:r0 + m, :]
        parts.append(bref - b[r0:r0 + m])
        parts.append(b[r0 + m:r0 + 2 * m] - bref)
    return jnp.concatenate(parts, axis=0)


def _select_rows(m, upper_val, lower_val):
    rows = upper_val.shape[0]
    if m >= 8:
        parts = []
        for r0 in range(0, rows, m):
            src = upper_val if (r0 // m) % 2 == 1 else lower_val
            parts.append(src[r0:r0 + m])
        return jnp.concatenate(parts, axis=0)
    row = lax.broadcasted_iota(jnp.int32, upper_val.shape, 0)
    return jnp.where((row & m) != 0, upper_val, lower_val)


def _mixer_kernel(h_ref, mod_ref, ng_ref, win_ref, wout_ref, lbl_ref, gn_ref, lng_ref, og_ref,
                  wsp_ref, bsp_ref, sums_ref, pair_ref, o_ref, st_ref, proj_ref, *, layer):
    @pl.when(pl.program_id(1) == 0)
    def _():
        st_ref[...] = jnp.zeros_like(st_ref)

    sh = mod_ref[0, 3:4, :]
    sc = mod_ref[0, 4:5, :]
    gate = mod_ref[0, 5:6, :]

    def normalize(r0):
        y = _rms(h_ref[0, r0:r0 + SUB_ROWS, :], RMS_EPS) * ng_ref[...]
        return (y * (1.0 + sc) + sh).astype(BF16)

    def proj(r0, rows, c0, c1):
        return proj_ref[r0:r0 + rows, c0:c1]

    def project(r0, y, c0, c1):
        proj_ref[r0:r0 + SUB_ROWS, c0:c1] = jnp.dot(y, win_ref[:, c0:c1], preferred_element_type=F32)

    def out_project(r0, cat, c0, c1):
        mix = jnp.dot(cat, wout_ref[:, c0:c1], preferred_element_type=F32)
        o_ref[0, r0:r0 + SUB_ROWS, c0:c1] = h_ref[0, r0:r0 + SUB_ROWS, c0:c1] + gate[:, c0:c1] * mix

    lg = lbl_ref[...]
    ex = jnp.exp(lg - jnp.max(lg, axis=0, keepdims=True))
    prob = ex / jnp.sum(ex, axis=0, keepdims=True)
    lb = jnp.zeros((1, HG_WIDTH), F32)
    for i in range(1, layer + 1):
        lb = lb + prob[i:i + 1, :]

    r2 = lax.broadcasted_iota(jnp.int32, (GM_BLOCK, GM_BLOCK), 0)
    c2 = lax.broadcasted_iota(jnp.int32, (GM_BLOCK, GM_BLOCK), 1)
    causal = (r2 // CHUNK) >= (c2 // CHUNK)
    w_sp = [jnp.where(causal, wsp_ref[hd], 0.0).astype(BF16) for hd in range(GM_HEADS)]

    def gm_mix(r0):
        res = []
        for hd in range(GM_HEADS):
            cu = 4 * HG_WIDTH + hd * GM_DH
            cv = cu + GM_WIDTH
            u = jax.nn.gelu(proj(r0, GM_BLOCK, cu, cu + GM_DH))
            v = jax.nn.gelu(proj(r0, GM_BLOCK, cv, cv + GM_DH))
            vc = v - jnp.mean(v, axis=-1, keepdims=True)
            var = jnp.mean(vc * vc, axis=-1, keepdims=True)
            vn = vc * lax.rsqrt(var + LN_EPS) * lng_ref[:, hd * GM_DH:(hd + 1) * GM_DH]
            mixed = jnp.dot(w_sp[hd], vn.astype(BF16), preferred_element_type=F32) + bsp_ref[hd]
            res.append((u, mixed))
        return res

    def gm_gate(res):
        outs = []
        for hd, (u, mixed) in enumerate(res):
            z = _rms(u * mixed, RMS_EPS) * og_ref[:, hd * GM_DH:(hd + 1) * GM_DH]
            outs.append(z.astype(BF16))
        return jnp.concatenate(outs, axis=1)

    def hg_gates(r0):
        q = proj(r0, CHUNK, 0, HG_WIDTH)
        f_raw = proj(r0, CHUNK, HG_WIDTH, 2 * HG_WIDTH)
        vb = proj(r0, CHUNK, 2 * HG_WIDTH, 3 * HG_WIDTH).astype(BF16)
        g = proj(r0, CHUNK, 3 * HG_WIDTH, 4 * HG_WIDTH)
        g_act = g * jax.nn.sigmoid(g)
        f = lb + (1.0 - lb) * jax.nn.sigmoid(f_raw)
        log_f = jnp.log(f) * LOG2_E
        k = 1.0 - f
        hi = log_f.astype(BF16)
        mid = (log_f - hi.astype(F32)).astype(BF16)
        sums = jnp.dot(sums_ref[...], jnp.concatenate([hi, mid], axis=0),
                       preferred_element_type=F32)
        return q, k, vb, g_act, sums

    def hg_scores(gates):
        q, k, vb_all, g_act, sums = gates
        b = sums[0:CHUNK]
        decay_in = jnp.exp2(b)
        decay_out = jnp.exp2(b[CHUNK - 1:CHUNK, :] - b)
        lvl = []
        for m in HALF_SIZES:
            if m in FINE_SIZES:
                fi = 1 + FINE_SIZES.index(m)
                log_decay = sums[fi * CHUNK:(fi + 1) * CHUNK]
            else:
                log_decay = _coarse_log_decay(b, m)
            lvl.append((_select_rows(m, q, k) * jnp.exp2(log_decay)).astype(BF16))
        qb = q.astype(BF16)
        kb = k.astype(BF16)
        q_in = (q * decay_in).astype(BF16)
        k_out = (k * decay_out).astype(BF16)
        dn = (((1,), (1,)), ((), ()))
        per_head = []
        for hd in range(HG_HEADS):
            cs = slice(hd * HG_DK, (hd + 1) * HG_DK)
            att = lax.dot_general(qb[:, cs], kb[:, cs], dn, preferred_element_type=F32) * pair_ref[N_LEVELS]
            for i in range(N_LEVELS):
                mh = lvl[i][:, cs]
                att = att + lax.dot_general(mh, mh, dn, preferred_element_type=F32) * pair_ref[i]
            vb = vb_all[:, cs]
            inc = lax.dot_general(vb, k_out[:, cs], (((0,), (0,)), ((), ())),
                                  preferred_element_type=F32)
            per_head.append((att.astype(BF16), vb, q_in[:, cs], inc, decay_in[CHUNK - 1:CHUNK, cs],
                             g_act[:, cs]))
        return per_head

    def hg_output(per_head):
        dn = (((1,), (1,)), ((), ()))
        outs = []
        for hd, (att, vb, q_in, inc, decay_last, g_act) in enumerate(per_head):
            cs = slice(hd * HG_DK, (hd + 1) * HG_DK)
            st = st_ref[hd]
            o = (jnp.dot(att, vb, preferred_element_type=F32)
                 + lax.dot_general(q_in, st.astype(BF16), dn, preferred_element_type=F32))
            st_ref[hd] = st * decay_last + inc
            o = _rms(o, RMS_EPS) * gn_ref[:, cs] * g_act
            outs.append(o.astype(BF16))
        return jnp.concatenate(outs, axis=1)

    n_sub = MIX_ROWS // SUB_ROWS
    n_chunks = MIX_ROWS // CHUNK
    chunks_per_sub = SUB_ROWS // CHUNK
    chunks_per_blk = GM_BLOCK // CHUNK
    in_piece = IN_COLS // chunks_per_sub
    out_piece = D_MODEL // chunks_per_sub
    lag_out = 2

    y = [normalize(s * SUB_ROWS) for s in range(n_sub)]
    project(0, y[0], 0, IN_COLS)
    gates, scores, gm_state, hg_out, gm_out, cat = {}, {}, {}, {}, {}, {}
    for slot in range(n_chunks + lag_out + chunks_per_sub):
        s_next = slot // chunks_per_sub + 1
        if s_next < n_sub:
            p = slot % chunks_per_sub
            project(s_next * SUB_ROWS, y[s_next], p * in_piece, (p + 1) * in_piece)
        done = slot - lag_out - chunks_per_sub
        if done >= 0 and done // chunks_per_sub < n_sub:
            s_done, p = divmod(done, chunks_per_sub)
            if p == 0:
                c0 = s_done * chunks_per_sub
                hg_part = jnp.concatenate([hg_out.pop(c0 + i) for i in range(chunks_per_sub)], axis=0)
                gm_part = jnp.concatenate(
                    [gm_out.pop(c0 + i) for i in range(0, chunks_per_sub, chunks_per_blk)], axis=0)
                cat[s_done] = jnp.concatenate([hg_part, gm_part], axis=1)
            out_project(s_done * SUB_ROWS, cat[s_done], p * out_piece, (p + 1) * out_piece)
        if slot < n_chunks:
            gates[slot] = hg_gates(slot * CHUNK)
            if slot % chunks_per_blk == 0:
                gm_state[slot] = gm_mix(slot * CHUNK)
            else:
                gm_out[slot - 1] = gm_gate(gm_state.pop(slot - 1))
        if 0 <= slot - 1 < n_chunks:
            scores[slot - 1] = hg_scores(gates.pop(slot - 1))
        if 0 <= slot - lag_out < n_chunks:
            hg_out[slot - lag_out] = hg_output(scores.pop(slot - lag_out))


def _mixer_call(h, mod, ngain, w_in, w_out, lb_logits, gnorm, ln_gain, out_gain, w_sp, b_sp,
                sums_c, pair_c, *, layer):
    batch = h.shape[0]
    depth = lb_logits.shape[0]
    kern = functools.partial(_mixer_kernel, layer=layer)
    return pl.pallas_call(
        kern,
        grid=(batch, SEQ // MIX_ROWS),
        in_specs=[
            pl.BlockSpec((1, MIX_ROWS, D_MODEL), lambda b, j: (b, j, 0)),
            pl.BlockSpec((1, N_MOD, D_MODEL), lambda b, j: (b, 0, 0)),
            _resident((1, D_MODEL)),
            _resident((D_MODEL, IN_COLS)),
            _resident((D_MODEL, D_MODEL)),
            _resident((depth, HG_WIDTH)),
            _resident((1, HG_WIDTH)),
            _resident((1, GM_WIDTH)),
            _resident((1, GM_WIDTH)),
            _resident((GM_HEADS, GM_BLOCK, GM_BLOCK)),
            _resident((GM_HEADS, GM_BLOCK, 1)),
            _resident(sums_c.shape),
            _resident(pair_c.shape),
        ],
        out_specs=pl.BlockSpec((1, MIX_ROWS, D_MODEL), lambda b, j: (b, j, 0)),
        out_shape=jax.ShapeDtypeStruct(h.shape, F32),
        scratch_shapes=[pltpu.VMEM((HG_HEADS, HG_DK, HG_DK), F32),
                        pltpu.VMEM((MIX_ROWS, IN_COLS), F32)],
        compiler_params=pltpu.CompilerParams(
            dimension_semantics=("arbitrary", "arbitrary"), vmem_limit_bytes=VMEM_LIMIT),
        name="mixer",
    )(h, mod, ngain, w_in, w_out, lb_logits, gnorm, ln_gain, out_gain, w_sp, b_sp, sums_c, pair_c)


def kernel(x, c, w_ada, b_ada, norm_gain, ffn1_w13, ffn1_w2, w_in, hg_lb_logits, hg_gnorm,
           gm_ln_gain, gm_w_spatial, gm_b_spatial, gm_out_gain, w_out, ffn2_w13, ffn2_w2,
           final_gain):
    batch, seq, _ = x.shape
    depth = w_ada.shape[0]
    sums_np, pair_np = _hgrn_constants()
    sums_c = jnp.asarray(sums_np, BF16)
    pair_c = jnp.asarray(pair_np, F32)
    fgain = final_gain.reshape(1, D_MODEL)

    mod_all = _ada_call(c, w_ada, b_ada).reshape(depth, batch, N_MOD, D_MODEL)
    h = x
    for l in range(depth):
        mod = mod_all[l]
        h = _ffn_call(h.reshape(batch * seq, D_MODEL), mod, norm_gain[l, 0:1],
                      ffn1_w13[l].astype(BF16), ffn1_w2[l].astype(BF16), fgain,
                      mod_base=0, final=False).reshape(batch, seq, D_MODEL)
        h = _mixer_call(h, mod, norm_gain[l, 1:2], w_in[l].astype(BF16), w_out[l].astype(BF16),
                        hg_lb_logits, hg_gnorm[l:l + 1], gm_ln_gain[l:l + 1], gm_out_gain[l:l + 1],
                        gm_w_spatial[l], gm_b_spatial[l].reshape(GM_HEADS, GM_BLOCK, 1),
                        sums_c, pair_c, layer=l)
        h = _ffn_call(h.reshape(batch * seq, D_MODEL), mod, norm_gain[l, 2:3],
                      ffn2_w13[l].astype(BF16), ffn2_w2[l].astype(BF16), fgain,
                      mod_base=6, final=(l == depth - 1)).reshape(batch, seq, D_MODEL)
    return h
```

```python
import functools

import numpy as np
import jax
import jax.numpy as jnp
from jax import lax
from jax.experimental import pallas as pl
from jax.experimental.pallas import tpu as pltpu

D_MODEL = 1024
SEQ = 2048
CHUNK = 64
HG_WIDTH = 512
HG_HEADS = 4
HG_DK = 128
GM_WIDTH = 512
GM_HEADS = 4
GM_DH = 128
GM_BLOCK = 128
D_FF = 2816
N_MOD = 9
IN_COLS = 4 * HG_WIDTH + 2 * GM_WIDTH
RMS_EPS = 1e-6
LN_EPS = 1e-5

FF_CHUNK = 256
N_FF_CHUNKS = D_FF // FF_CHUNK
W13_BLOCK = 2 * FF_CHUNK
WIN_BLOCK = 768
WOUT_BLOCK = 256
FFN_ROWS = 512
MIX_ROWS = 512
SUB_ROWS = 256
ADA_COLS = 1024
HALF_SIZES = (32, 16, 8, 4, 2, 1)
N_LEVELS = len(HALF_SIZES)
FINE_SIZES = (4, 2, 1)
VMEM_LIMIT = 56 * 1024 * 1024
LOG2_E = 1.4426950408889634

F32 = jnp.float32
BF16 = jnp.bfloat16


def _rms(x, eps):
    return x * lax.rsqrt(jnp.mean(x * x, axis=-1, keepdims=True) + eps)


def _resident(shape):
    return pl.BlockSpec(shape, lambda *_: (0,) * len(shape), pipeline_mode=pl.Buffered(1))


def _ada_kernel(c_ref, w_ref, b_ref, o_ref):
    c = c_ref[...]
    ca = c * jax.nn.sigmoid(c)
    batch = ca.shape[0]
    ca_hi = ca.astype(BF16)
    ca_lo = (ca - ca_hi.astype(F32)).astype(BF16)
    w = w_ref[0]
    w_hi = w.astype(BF16)
    w_lo = (w - w_hi.astype(F32)).astype(BF16)
    p = jnp.dot(jnp.concatenate([ca_hi, ca_lo], axis=0), w_hi, preferred_element_type=F32)
    q = jnp.dot(ca_hi, w_lo, preferred_element_type=F32)
    o_ref[0] = p[:batch] + p[batch:] + q + b_ref[0]


def _ada_call(c, w_ada, b_ada):
    depth, _, n_out = w_ada.shape
    batch = c.shape[0]
    return pl.pallas_call(
        _ada_kernel,
        grid=(depth, n_out // ADA_COLS),
        in_specs=[
            pl.BlockSpec((batch, D_MODEL), lambda l, j: (0, 0)),
            pl.BlockSpec((1, D_MODEL, ADA_COLS), lambda l, j: (l, 0, j)),
            pl.BlockSpec((1, 1, ADA_COLS), lambda l, j: (l, 0, j)),
        ],
        out_specs=pl.BlockSpec((1, batch, ADA_COLS), lambda l, j: (l, 0, j)),
        out_shape=jax.ShapeDtypeStruct((depth, batch, n_out), F32),
        compiler_params=pltpu.CompilerParams(
            dimension_semantics=("arbitrary", "arbitrary"), vmem_limit_bytes=VMEM_LIMIT),
        name="ada",
    )(c, w_ada, b_ada.reshape(depth, 1, n_out))


def _stream_cast(n_blocks, src_block, dst_ref, stage_ref, sem_ref):
    def copy(i, slot):
        return pltpu.make_async_copy(src_block(i), stage_ref.at[slot], sem_ref.at[slot])

    copy(0, 0).start()

    def body(i, carry):
        slot = i % 2

        @pl.when(i + 1 < n_blocks)
        def _():
            copy(i + 1, 1 - slot).start()

        copy(i, slot).wait()
        dst_ref[i] = stage_ref[slot].astype(BF16)
        return carry

    lax.fori_loop(0, n_blocks, body, 0)


def _col_block(w_hbm, layer, width):
    return lambda i: w_hbm.at[layer, :, pl.ds(pl.multiple_of(i * width, width), width)]


def _row_block(w_hbm, layer, height):
    return lambda i: w_hbm.at[layer, pl.ds(pl.multiple_of(i * height, height), height), :]


def _cols(w_ref, c0, width):
    block_width = w_ref.shape[2]
    blk, off = divmod(c0, block_width)
    assert off + width <= block_width
    return w_ref[blk, :, off:off + width]


def _ffn_kernel(h_ref, mod_ref, ng_ref, w13_hbm, w2_hbm, fg_ref, o_ref, w13_ref, w2_ref,
                stage13_ref, stage2_ref, sem13_ref, sem2_ref, *, layer, mod_base, final):
    @pl.when(pl.program_id(0) == 0)
    def _():
        _stream_cast(w13_ref.shape[0], _col_block(w13_hbm, layer, W13_BLOCK), w13_ref,
                     stage13_ref, sem13_ref)
        _stream_cast(w2_ref.shape[0], _row_block(w2_hbm, layer, FF_CHUNK), w2_ref,
                     stage2_ref, sem2_ref)

    h = h_ref[...]
    sh = mod_ref[0, mod_base:mod_base + 1, :]
    sc = mod_ref[0, mod_base + 1:mod_base + 2, :]
    gate = mod_ref[0, mod_base + 2:mod_base + 3, :]
    y = _rms(h, RMS_EPS) * ng_ref[...]
    yb = (y * (1.0 + sc) + sh).astype(BF16)
    acc = None
    for j in range(N_FF_CHUNKS):
        c0 = j * FF_CHUNK
        a = jnp.dot(yb, _cols(w13_ref, c0, FF_CHUNK), preferred_element_type=F32)
        b = jnp.dot(yb, _cols(w13_ref, D_FF + c0, FF_CHUNK), preferred_element_type=F32)
        hm = (a * jax.nn.sigmoid(a) * b).astype(BF16)
        d = jnp.dot(hm, w2_ref[j], preferred_element_type=F32)
        acc = d if acc is None else acc + d
    res = h_ref[...] + 0.5 * gate * acc
    if final:
        res = _rms(res, RMS_EPS) * fg_ref[...]
    o_ref[...] = res


def _ffn_call(h2d, mod, ngain, w13, w2, fgain, *, layer, mod_base, final):
    n_tok = h2d.shape[0]
    tiles_per_seq = SEQ // FFN_ROWS
    kern = functools.partial(_ffn_kernel, layer=layer, mod_base=mod_base, final=final)
    return pl.pallas_call(
        kern,
        grid=(n_tok // FFN_ROWS,),
        in_specs=[
            pl.BlockSpec((FFN_ROWS, D_MODEL), lambda i: (i, 0)),
            pl.BlockSpec((1, N_MOD, D_MODEL), lambda i: (i // tiles_per_seq, 0, 0)),
            _resident((1, D_MODEL)),
            pl.BlockSpec(memory_space=pl.ANY),
            pl.BlockSpec(memory_space=pl.ANY),
            _resident((1, D_MODEL)),
        ],
        out_specs=pl.BlockSpec((FFN_ROWS, D_MODEL), lambda i: (i, 0)),
        out_shape=jax.ShapeDtypeStruct(h2d.shape, F32),
        scratch_shapes=[
            pltpu.VMEM((2 * D_FF // W13_BLOCK, D_MODEL, W13_BLOCK), BF16),
            pltpu.VMEM((N_FF_CHUNKS, FF_CHUNK, D_MODEL), BF16),
            pltpu.VMEM((2, D_MODEL, W13_BLOCK), F32),
            pltpu.VMEM((2, FF_CHUNK, D_MODEL), F32),
            pltpu.SemaphoreType.DMA((2,)),
            pltpu.SemaphoreType.DMA((2,)),
        ],
        compiler_params=pltpu.CompilerParams(
            dimension_semantics=("arbitrary",), vmem_limit_bytes=VMEM_LIMIT),
        name="ffn_final" if final else "ffn",
    )(h2d, mod, ngain, w13, w2, fgain)


def _hgrn_constants():
    r = np.arange(CHUNK)
    sums = np.zeros((1 + len(FINE_SIZES), CHUNK, CHUNK), np.float32)
    pair = np.zeros((N_LEVELS + 1, CHUNK, CHUNK), np.float32)
    sums[0] = r[None, :] <= r[:, None]
    for i, m in enumerate(HALF_SIZES):
        blk = r // (2 * m)
        low = (r % (2 * m)) < m
        bnd = blk * 2 * m + m - 1
        pair[i] = (blk[:, None] == blk[None, :]) & (~low)[:, None] & low[None, :]
        if m in FINE_SIZES:
            fi = 1 + FINE_SIZES.index(m)
            for t in range(CHUNK):
                if low[t]:
                    sums[fi, t, t + 1:bnd[t] + 1] = 1.0
                else:
                    sums[fi, t, bnd[t] + 1:t + 1] = 1.0
    pair[N_LEVELS] = np.eye(CHUNK)
    sums = sums.reshape(-1, CHUNK)
    return np.concatenate([sums, sums], axis=1), pair


def _coarse_log_decay(b, m):
    parts = []
    for r0 in range(0, b.shape[0], 2 * m):
        bref = b[r0 + m - 1:r0 + m, :]
        parts.append(bref - b[r0:r0 + m])
        parts.append(b[r0 + m:r0 + 2 * m] - bref)
    return jnp.concatenate(parts, axis=0)


def _select_rows(m, upper_val, lower_val):
    rows = upper_val.shape[0]
    if m >= 8:
        parts = []
        for r0 in range(0, rows, m):
            src = upper_val if (r0 // m) % 2 == 1 else lower_val
            parts.append(src[r0:r0 + m])
        return jnp.concatenate(parts, axis=0)
    row = lax.broadcasted_iota(jnp.int32, upper_val.shape, 0)
    return jnp.where((row & m) != 0, upper_val, lower_val)


def _mixer_kernel(h_ref, mod_ref, ng_ref, win_hbm, wout_hbm, lbl_ref, gn_ref, lng_ref, og_ref,
                  wsp_ref, bsp_ref, sums_ref, pair_ref, o_ref, st_ref, proj_ref, win_ref, wout_ref,
                  stage_in_ref, stage_out_ref, sem_in_ref, sem_out_ref, *, layer):
    @pl.when((pl.program_id(0) == 0) & (pl.program_id(1) == 0))
    def _():
        _stream_cast(win_ref.shape[0], _col_block(win_hbm, layer, WIN_BLOCK), win_ref,
                     stage_in_ref, sem_in_ref)
        _stream_cast(wout_ref.shape[0], _col_block(wout_hbm, layer, WOUT_BLOCK), wout_ref,
                     stage_out_ref, sem_out_ref)

    @pl.when(pl.program_id(1) == 0)
    def _():
        st_ref[...] = jnp.zeros_like(st_ref)

    sh = mod_ref[0, 3:4, :]
    sc = mod_ref[0, 4:5, :]
    gate = mod_ref[0, 5:6, :]

    def normalize(r0):
        y = _rms(h_ref[0, r0:r0 + SUB_ROWS, :], RMS_EPS) * ng_ref[...]
        return (y * (1.0 + sc) + sh).astype(BF16)

    def proj(r0, rows, c0, c1):
        return proj_ref[r0:r0 + rows, c0:c1]

    def project(r0, y, c0, c1):
        for c in range(c0, c1, WIN_BLOCK):
            proj_ref[r0:r0 + SUB_ROWS, c:c + WIN_BLOCK] = jnp.dot(
                y, _cols(win_ref, c, WIN_BLOCK), preferred_element_type=F32)

    def out_project(r0, cat, c0, c1):
        mix = jnp.dot(cat, _cols(wout_ref, c0, c1 - c0), preferred_element_type=F32)
        o_ref[0, r0:r0 + SUB_ROWS, c0:c1] = h_ref[0, r0:r0 + SUB_ROWS, c0:c1] + gate[:, c0:c1] * mix

    lg = lbl_ref[...]
    ex = jnp.exp(lg - jnp.max(lg, axis=0, keepdims=True))
    prob = ex / jnp.sum(ex, axis=0, keepdims=True)
    lb = jnp.zeros((1, HG_WIDTH), F32)
    for i in range(1, layer + 1):
        lb = lb + prob[i:i + 1, :]

    r2 = lax.broadcasted_iota(jnp.int32, (GM_BLOCK, GM_BLOCK), 0)
    c2 = lax.broadcasted_iota(jnp.int32, (GM_BLOCK, GM_BLOCK), 1)
    causal = (r2 // CHUNK) >= (c2 // CHUNK)
    w_sp = [jnp.where(causal, wsp_ref[hd], 0.0).astype(BF16) for hd in range(GM_HEADS)]

    def gm_mix(r0):
        res = []
        for hd in range(GM_HEADS):
            cu = 4 * HG_WIDTH + hd * GM_DH
            cv = cu + GM_WIDTH
            u = jax.nn.gelu(proj(r0, GM_BLOCK, cu, cu + GM_DH))
            v = jax.nn.gelu(proj(r0, GM_BLOCK, cv, cv + GM_DH))
            vc = v - jnp.mean(v, axis=-1, keepdims=True)
            var = jnp.mean(vc * vc, axis=-1, keepdims=True)
            vn = vc * lax.rsqrt(var + LN_EPS) * lng_ref[:, hd * GM_DH:(hd + 1) * GM_DH]
            mixed = jnp.dot(w_sp[hd], vn.astype(BF16), preferred_element_type=F32) + bsp_ref[hd]
            res.append((u, mixed))
        return res

    def gm_gate(res):
        outs = []
        for hd, (u, mixed) in enumerate(res):
            z = _rms(u * mixed, RMS_EPS) * og_ref[:, hd * GM_DH:(hd + 1) * GM_DH]
            outs.append(z.astype(BF16))
        return jnp.concatenate(outs, axis=1)

    def hg_gates(r0):
        q = proj(r0, CHUNK, 0, HG_WIDTH)
        f_raw = proj(r0, CHUNK, HG_WIDTH, 2 * HG_WIDTH)
        vb = proj(r0, CHUNK, 2 * HG_WIDTH, 3 * HG_WIDTH).astype(BF16)
        g = proj(r0, CHUNK, 3 * HG_WIDTH, 4 * HG_WIDTH)
        g_act = g * jax.nn.sigmoid(g)
        f = lb + (1.0 - lb) * jax.nn.sigmoid(f_raw)
        log_f = jnp.log(f) * LOG2_E
        k = 1.0 - f
        hi = log_f.astype(BF16)
        mid = (log_f - hi.astype(F32)).astype(BF16)
        sums = jnp.dot(sums_ref[...], jnp.concatenate([hi, mid], axis=0),
                       preferred_element_type=F32)
        return q, k, vb, g_act, sums

    def hg_scores(gates):
        q, k, vb_all, g_act, sums = gates
        b = sums[0:CHUNK]
        decay_in = jnp.exp2(b)
        decay_out = jnp.exp2(b[CHUNK - 1:CHUNK, :] - b)
        lvl = []
        for m in HALF_SIZES:
            if m in FINE_SIZES:
                fi = 1 + FINE_SIZES.index(m)
                log_decay = sums[fi * CHUNK:(fi + 1) * CHUNK]
            else:
                log_decay = _coarse_log_decay(b, m)
            lvl.append((_select_rows(m, q, k) * jnp.exp2(log_decay)).astype(BF16))
        qb = q.astype(BF16)
        kb = k.astype(BF16)
        q_in = (q * decay_in).astype(BF16)
        k_out = (k * decay_out).astype(BF16)
        dn = (((1,), (1,)), ((), ()))
        per_head = []
        for hd in range(HG_HEADS):
            cs = slice(hd * HG_DK, (hd + 1) * HG_DK)
            att = lax.dot_general(qb[:, cs], kb[:, cs], dn, preferred_element_type=F32) * pair_ref[N_LEVELS]
            for i in range(N_LEVELS):
                mh = lvl[i][:, cs]
                att = att + lax.dot_general(mh, mh, dn, preferred_element_type=F32) * pair_ref[i]
            vb = vb_all[:, cs]
            inc = lax.dot_general(vb, k_out[:, cs], (((0,), (0,)), ((), ())),
                                  preferred_element_type=F32)
            per_head.append((att.astype(BF16), vb, q_in[:, cs], inc, decay_in[CHUNK - 1:CHUNK, cs],
                             g_act[:, cs]))
        return per_head

    def hg_output(per_head):
        dn = (((1,), (1,)), ((), ()))
        outs = []
        for hd, (att, vb, q_in, inc, decay_last, g_act) in enumerate(per_head):
            cs = slice(hd * HG_DK, (hd + 1) * HG_DK)
            st = st_ref[hd]
            o = (jnp.dot(att, vb, preferred_element_type=F32)
                 + lax.dot_general(q_in, st.astype(BF16), dn, preferred_element_type=F32))
            st_ref[hd] = st * decay_last + inc
            o = _rms(o, RMS_EPS) * gn_ref[:, cs] * g_act
            outs.append(o.astype(BF16))
        return jnp.concatenate(outs, axis=1)

    n_sub = MIX_ROWS // SUB_ROWS
    n_chunks = MIX_ROWS // CHUNK
    chunks_per_sub = SUB_ROWS // CHUNK
    chunks_per_blk = GM_BLOCK // CHUNK
    in_piece = IN_COLS // chunks_per_sub
    out_piece = D_MODEL // chunks_per_sub
    lag_out = 2

    y = [normalize(s * SUB_ROWS) for s in range(n_sub)]
    project(0, y[0], 0, IN_COLS)
    gates, scores, gm_state, hg_out, gm_out, cat = {}, {}, {}, {}, {}, {}
    for slot in range(n_chunks + lag_out + chunks_per_sub):
        s_next = slot // chunks_per_sub + 1
        if s_next < n_sub:
            p = slot % chunks_per_sub
            project(s_next * SUB_ROWS, y[s_next], p * in_piece, (p + 1) * in_piece)
        done = slot - lag_out - chunks_per_sub
        if done >= 0 and done // chunks_per_sub < n_sub:
            s_done, p = divmod(done, chunks_per_sub)
            if p == 0:
                c0 = s_done * chunks_per_sub
                hg_part = jnp.concatenate([hg_out.pop(c0 + i) for i in range(chunks_per_sub)], axis=0)
                gm_part = jnp.concatenate(
                    [gm_out.pop(c0 + i) for i in range(0, chunks_per_sub, chunks_per_blk)], axis=0)
                cat[s_done] = jnp.concatenate([hg_part, gm_part], axis=1)
            out_project(s_done * SUB_ROWS, cat[s_done], p * out_piece, (p + 1) * out_piece)
        if slot < n_chunks:
            gates[slot] = hg_gates(slot * CHUNK)
            if slot % chunks_per_blk == 0:
                gm_state[slot] = gm_mix(slot * CHUNK)
            else:
                gm_out[slot - 1] = gm_gate(gm_state.pop(slot - 1))
        if 0 <= slot - 1 < n_chunks:
            scores[slot - 1] = hg_scores(gates.pop(slot - 1))
        if 0 <= slot - lag_out < n_chunks:
            hg_out[slot - lag_out] = hg_output(scores.pop(slot - lag_out))


def _mixer_call(h, mod, ngain, w_in, w_out, lb_logits, gnorm, ln_gain, out_gain, w_sp, b_sp,
                sums_c, pair_c, *, layer):
    batch = h.shape[0]
    depth = lb_logits.shape[0]
    kern = functools.partial(_mixer_kernel, layer=layer)
    return pl.pallas_call(
        kern,
        grid=(batch, SEQ // MIX_ROWS),
        in_specs=[
            pl.BlockSpec((1, MIX_ROWS, D_MODEL), lambda b, j: (b, j, 0)),
            pl.BlockSpec((1, N_MOD, D_MODEL), lambda b, j: (b, 0, 0)),
            _resident((1, D_MODEL)),
            pl.BlockSpec(memory_space=pl.ANY),
            pl.BlockSpec(memory_space=pl.ANY),
            _resident((depth, HG_WIDTH)),
            _resident((1, HG_WIDTH)),
            _resident((1, GM_WIDTH)),
            _resident((1, GM_WIDTH)),
            _resident((GM_HEADS, GM_BLOCK, GM_BLOCK)),
            _resident((GM_HEADS, GM_BLOCK, 1)),
            _resident(sums_c.shape),
            _resident(pair_c.shape),
        ],
        out_specs=pl.BlockSpec((1, MIX_ROWS, D_MODEL), lambda b, j: (b, j, 0)),
        out_shape=jax.ShapeDtypeStruct(h.shape, F32),
        scratch_shapes=[pltpu.VMEM((HG_HEADS, HG_DK, HG_DK), F32),
                        pltpu.VMEM((MIX_ROWS, IN_COLS), F32),
                        pltpu.VMEM((IN_COLS // WIN_BLOCK, D_MODEL, WIN_BLOCK), BF16),
                        pltpu.VMEM((D_MODEL // WOUT_BLOCK, D_MODEL, WOUT_BLOCK), BF16),
                        pltpu.VMEM((2, D_MODEL, WIN_BLOCK), F32),
                        pltpu.VMEM((2, D_MODEL, WOUT_BLOCK), F32),
                        pltpu.SemaphoreType.DMA((2,)),
                        pltpu.SemaphoreType.DMA((2,))],
        compiler_params=pltpu.CompilerParams(
            dimension_semantics=("arbitrary", "arbitrary"), vmem_limit_bytes=VMEM_LIMIT),
        name="mixer",
    )(h, mod, ngain, w_in, w_out, lb_logits, gnorm, ln_gain, out_gain, w_sp, b_sp, sums_c, pair_c)


def kernel(x, c, w_ada, b_ada, norm_gain, ffn1_w13, ffn1_w2, w_in, hg_lb_logits, hg_gnorm,
           gm_ln_gain, gm_w_spatial, gm_b_spatial, gm_out_gain, w_out, ffn2_w13, ffn2_w2,
           final_gain):
    batch, seq, _ = x.shape
    depth = w_ada.shape[0]
    sums_np, pair_np = _hgrn_constants()
    sums_c = jnp.asarray(sums_np, BF16)
    pair_c = jnp.asarray(pair_np, F32)
    fgain = final_gain.reshape(1, D_MODEL)

    mod_all = _ada_call(c, w_ada, b_ada).reshape(depth, batch, N_MOD, D_MODEL)
    h = x
    for l in range(depth):
        mod = mod_all[l]
        h = _ffn_call(h.reshape(batch * seq, D_MODEL), mod, norm_gain[l, 0:1], ffn1_w13, ffn1_w2,
                      fgain, layer=l, mod_base=0, final=False).reshape(batch, seq, D_MODEL)
        h = _mixer_call(h, mod, norm_gain[l, 1:2], w_in, w_out,
                        hg_lb_logits, hg_gnorm[l:l + 1], gm_ln_gain[l:l + 1], gm_out_gain[l:l + 1],
                        gm_w_spatial[l], gm_b_spatial[l].reshape(GM_HEADS, GM_BLOCK, 1),
                        sums_c, pair_c, layer=l)
        h = _ffn_call(h.reshape(batch * seq, D_MODEL), mod, norm_gain[l, 2:3], ffn2_w13, ffn2_w2,
                      fgain, layer=l, mod_base=6,
                      final=(l == depth - 1)).reshape(batch, seq, D_MODEL)
    return h
```

```python
import functools

import numpy as np
import jax
import jax.numpy as jnp
from jax import lax
from jax.experimental import pallas as pl
from jax.experimental.pallas import tpu as pltpu

D_MODEL = 1024
SEQ = 2048
CHUNK = 64
HG_WIDTH = 512
HG_HEADS = 4
HG_DK = 128
GM_WIDTH = 512
GM_HEADS = 4
GM_DH = 128
GM_BLOCK = 128
D_FF = 2816
N_MOD = 9
IN_COLS = 4 * HG_WIDTH + 2 * GM_WIDTH
RMS_EPS = 1e-6
LN_EPS = 1e-5

FF_CHUNK = 256
N_FF_CHUNKS = D_FF // FF_CHUNK
W13_BLOCK = 2 * FF_CHUNK
WIN_BLOCK = 768
WOUT_BLOCK = 256
FFN_ROWS = 512
MIX_ROWS = 512
SUB_ROWS = 256
ADA_COLS = 1024
HALF_SIZES = (32, 16, 8, 4, 2, 1)
N_LEVELS = len(HALF_SIZES)
SUBLANES = 8
VMEM_LIMIT = 56 * 1024 * 1024
LOG2_E = 1.4426950408889634

F32 = jnp.float32
BF16 = jnp.bfloat16


def _rms(x, eps):
    return x * lax.rsqrt(jnp.mean(x * x, axis=-1, keepdims=True) + eps)


def _resident(shape):
    return pl.BlockSpec(shape, lambda *_: (0,) * len(shape), pipeline_mode=pl.Buffered(1))


def _ada_kernel(c_ref, w_ref, b_ref, o_ref):
    c = c_ref[...]
    ca = c * jax.nn.sigmoid(c)
    batch = ca.shape[0]
    ca_hi = ca.astype(BF16)
    ca_lo = (ca - ca_hi.astype(F32)).astype(BF16)
    w = w_ref[0]
    w_hi = w.astype(BF16)
    w_lo = (w - w_hi.astype(F32)).astype(BF16)
    p = jnp.dot(jnp.concatenate([ca_hi, ca_lo], axis=0), w_hi, preferred_element_type=F32)
    q = jnp.dot(ca_hi, w_lo, preferred_element_type=F32)
    o_ref[0] = p[:batch] + p[batch:] + q + b_ref[0]


def _ada_call(c, w_ada, b_ada):
    depth, _, n_out = w_ada.shape
    batch = c.shape[0]
    return pl.pallas_call(
        _ada_kernel,
        grid=(depth, n_out // ADA_COLS),
        in_specs=[
            pl.BlockSpec((batch, D_MODEL), lambda l, j: (0, 0)),
            pl.BlockSpec((1, D_MODEL, ADA_COLS), lambda l, j: (l, 0, j)),
            pl.BlockSpec((1, 1, ADA_COLS), lambda l, j: (l, 0, j)),
        ],
        out_specs=pl.BlockSpec((1, batch, ADA_COLS), lambda l, j: (l, 0, j)),
        out_shape=jax.ShapeDtypeStruct((depth, batch, n_out), F32),
        compiler_params=pltpu.CompilerParams(
            dimension_semantics=("arbitrary", "arbitrary"), vmem_limit_bytes=VMEM_LIMIT),
        name="ada",
    )(c, w_ada, b_ada.reshape(depth, 1, n_out))


def _stream_cast(n_blocks, src_block, dst_ref, stage_ref, sem_ref):
    def copy(i, slot):
        return pltpu.make_async_copy(src_block(i), stage_ref.at[slot], sem_ref.at[slot])

    copy(0, 0).start()

    def body(i, carry):
        slot = i % 2

        @pl.when(i + 1 < n_blocks)
        def _():
            copy(i + 1, 1 - slot).start()

        copy(i, slot).wait()
        dst_ref[i] = stage_ref[slot].astype(BF16)
        return carry

    lax.fori_loop(0, n_blocks, body, 0)


def _col_block(w_hbm, layer, width):
    return lambda i: w_hbm.at[layer, :, pl.ds(pl.multiple_of(i * width, width), width)]


def _row_block(w_hbm, layer, height):
    return lambda i: w_hbm.at[layer, pl.ds(pl.multiple_of(i * height, height), height), :]


def _cols(w_ref, c0, width):
    block_width = w_ref.shape[2]
    blk, off = divmod(c0, block_width)
    assert off + width <= block_width
    return w_ref[blk, :, off:off + width]


def _ffn_kernel(h_ref, mod_ref, ng_ref, w13_hbm, w2_hbm, fg_ref, o_ref, w13_ref, w2_ref,
                stage13_ref, stage2_ref, sem13_ref, sem2_ref, *, layer, mod_base, final):
    @pl.when(pl.program_id(0) == 0)
    def _():
        _stream_cast(w13_ref.shape[0], _col_block(w13_hbm, layer, W13_BLOCK), w13_ref,
                     stage13_ref, sem13_ref)
        _stream_cast(w2_ref.shape[0], _row_block(w2_hbm, layer, FF_CHUNK), w2_ref,
                     stage2_ref, sem2_ref)

    h = h_ref[...]
    sh = mod_ref[0, mod_base:mod_base + 1, :]
    sc = mod_ref[0, mod_base + 1:mod_base + 2, :]
    gate = mod_ref[0, mod_base + 2:mod_base + 3, :]
    y = _rms(h, RMS_EPS) * ng_ref[...]
    yb = (y * (1.0 + sc) + sh).astype(BF16)
    acc = None
    for j in range(N_FF_CHUNKS):
        c0 = j * FF_CHUNK
        a = jnp.dot(yb, _cols(w13_ref, c0, FF_CHUNK), preferred_element_type=F32)
        b = jnp.dot(yb, _cols(w13_ref, D_FF + c0, FF_CHUNK), preferred_element_type=F32)
        hm = (a * jax.nn.sigmoid(a) * b).astype(BF16)
        d = jnp.dot(hm, w2_ref[j], preferred_element_type=F32)
        acc = d if acc is None else acc + d
    res = h_ref[...] + 0.5 * gate * acc
    if final:
        res = _rms(res, RMS_EPS) * fg_ref[...]
    o_ref[...] = res


def _ffn_call(h2d, mod, ngain, w13, w2, fgain, *, layer, mod_base, final):
    n_tok = h2d.shape[0]
    tiles_per_seq = SEQ // FFN_ROWS
    kern = functools.partial(_ffn_kernel, layer=layer, mod_base=mod_base, final=final)
    return pl.pallas_call(
        kern,
        grid=(n_tok // FFN_ROWS,),
        in_specs=[
            pl.BlockSpec((FFN_ROWS, D_MODEL), lambda i: (i, 0)),
            pl.BlockSpec((1, N_MOD, D_MODEL), lambda i: (i // tiles_per_seq, 0, 0)),
            _resident((1, D_MODEL)),
            pl.BlockSpec(memory_space=pl.ANY),
            pl.BlockSpec(memory_space=pl.ANY),
            _resident((1, D_MODEL)),
        ],
        out_specs=pl.BlockSpec((FFN_ROWS, D_MODEL), lambda i: (i, 0)),
        out_shape=jax.ShapeDtypeStruct(h2d.shape, F32),
        scratch_shapes=[
            pltpu.VMEM((2 * D_FF // W13_BLOCK, D_MODEL, W13_BLOCK), BF16),
            pltpu.VMEM((N_FF_CHUNKS, FF_CHUNK, D_MODEL), BF16),
            pltpu.VMEM((2, D_MODEL, W13_BLOCK), F32),
            pltpu.VMEM((2, FF_CHUNK, D_MODEL), F32),
            pltpu.SemaphoreType.DMA((2,)),
            pltpu.SemaphoreType.DMA((2,)),
        ],
        compiler_params=pltpu.CompilerParams(
            dimension_semantics=("arbitrary",), vmem_limit_bytes=VMEM_LIMIT),
        name="ffn_final" if final else "ffn",
    )(h2d, mod, ngain, w13, w2, fgain)


def _hgrn_constants():
    r = np.arange(CHUNK)
    pair = np.zeros((N_LEVELS + 1, CHUNK, CHUNK), np.float32)
    prefix = (r[None, :] <= r[:, None]).astype(np.float32)
    for i, m in enumerate(HALF_SIZES):
        blk = r // (2 * m)
        low = (r % (2 * m)) < m
        pair[i] = (blk[:, None] == blk[None, :]) & (~low)[:, None] & low[None, :]
    pair[N_LEVELS] = np.eye(CHUNK)
    return np.concatenate([prefix, prefix], axis=1), pair


def _level_log_decay(b, m):
    rows, width = b.shape
    if m >= SUBLANES:
        parts = []
        for r0 in range(0, rows, 2 * m):
            bref = b[r0 + m - 1:r0 + m, :]
            parts.append(bref - b[r0:r0 + m])
            parts.append(b[r0 + m:r0 + 2 * m] - bref)
        return jnp.concatenate(parts, axis=0)
    row = lax.broadcasted_iota(jnp.int32, b.shape, 0)
    if m == 1:
        bref = pltpu.roll(b, 1, 0)
    else:
        def group_row(i):
            return jnp.concatenate(
                [jnp.broadcast_to(b[g + i:g + i + 1, :], (SUBLANES, width))
                 for g in range(0, rows, SUBLANES)], axis=0)
        bref = group_row(m - 1)
        for blk in range(1, SUBLANES // (2 * m)):
            bref = jnp.where((row % SUBLANES) >= blk * 2 * m, group_row(blk * 2 * m + m - 1), bref)
    upper = (row & m) != 0
    if m == 1:
        return jnp.where(upper, b - bref, 0.0)
    return jnp.where(upper, b - bref, bref - b)


def _select_rows(m, upper_val, lower_val):
    rows = upper_val.shape[0]
    if m >= SUBLANES:
        parts = []
        for r0 in range(0, rows, m):
            src = upper_val if (r0 // m) % 2 == 1 else lower_val
            parts.append(src[r0:r0 + m])
        return jnp.concatenate(parts, axis=0)
    row = lax.broadcasted_iota(jnp.int32, upper_val.shape, 0)
    return jnp.where((row & m) != 0, upper_val, lower_val)


def _mixer_kernel(h_ref, mod_ref, ng_ref, win_hbm, wout_hbm, lbl_ref, gn_ref, lng_ref, og_ref,
                  wsp_ref, bsp_ref, prefix_ref, pair_ref, o_ref, st_ref, proj_ref, win_ref, wout_ref,
                  stage_in_ref, stage_out_ref, sem_in_ref, sem_out_ref, *, layer):
    @pl.when((pl.program_id(0) == 0) & (pl.program_id(1) == 0))
    def _():
        _stream_cast(win_ref.shape[0], _col_block(win_hbm, layer, WIN_BLOCK), win_ref,
                     stage_in_ref, sem_in_ref)
        _stream_cast(wout_ref.shape[0], _col_block(wout_hbm, layer, WOUT_BLOCK), wout_ref,
                     stage_out_ref, sem_out_ref)

    @pl.when(pl.program_id(1) == 0)
    def _():
        st_ref[...] = jnp.zeros_like(st_ref)

    sh = mod_ref[0, 3:4, :]
    sc = mod_ref[0, 4:5, :]
    gate = mod_ref[0, 5:6, :]

    def normalize(r0):
        y = _rms(h_ref[0, r0:r0 + SUB_ROWS, :], RMS_EPS) * ng_ref[...]
        return (y * (1.0 + sc) + sh).astype(BF16)

    def proj(r0, rows, c0, c1):
        return proj_ref[r0:r0 + rows, c0:c1]

    def project(r0, y, c0, c1):
        for c in range(c0, c1, WIN_BLOCK):
            proj_ref[r0:r0 + SUB_ROWS, c:c + WIN_BLOCK] = jnp.dot(
                y, _cols(win_ref, c, WIN_BLOCK), preferred_element_type=F32)

    def out_project(r0, cat, c0, c1):
        mix = jnp.dot(cat, _cols(wout_ref, c0, c1 - c0), preferred_element_type=F32)
        o_ref[0, r0:r0 + SUB_ROWS, c0:c1] = h_ref[0, r0:r0 + SUB_ROWS, c0:c1] + gate[:, c0:c1] * mix

    lg = lbl_ref[...]
    ex = jnp.exp(lg - jnp.max(lg, axis=0, keepdims=True))
    prob = ex / jnp.sum(ex, axis=0, keepdims=True)
    lb = jnp.zeros((1, HG_WIDTH), F32)
    for i in range(1, layer + 1):
        lb = lb + prob[i:i + 1, :]

    r2 = lax.broadcasted_iota(jnp.int32, (GM_BLOCK, GM_BLOCK), 0)
    c2 = lax.broadcasted_iota(jnp.int32, (GM_BLOCK, GM_BLOCK), 1)
    causal = (r2 // CHUNK) >= (c2 // CHUNK)
    w_sp = [jnp.where(causal, wsp_ref[hd], 0.0).astype(BF16) for hd in range(GM_HEADS)]

    def gm_mix(r0, n_blocks):
        res = [[] for _ in range(n_blocks)]
        for hd in range(GM_HEADS):
            cu = 4 * HG_WIDTH + hd * GM_DH
            cv = cu + GM_WIDTH
            us, vns = [], []
            for blk in range(n_blocks):
                rb = r0 + blk * GM_BLOCK
                us.append(jax.nn.gelu(proj(rb, GM_BLOCK, cu, cu + GM_DH)))
                v = jax.nn.gelu(proj(rb, GM_BLOCK, cv, cv + GM_DH))
                vc = v - jnp.mean(v, axis=-1, keepdims=True)
                var = jnp.mean(vc * vc, axis=-1, keepdims=True)
                vn = vc * lax.rsqrt(var + LN_EPS) * lng_ref[:, hd * GM_DH:(hd + 1) * GM_DH]
                vns.append(vn.astype(BF16))
            mixed = jnp.dot(w_sp[hd], jnp.concatenate(vns, axis=1), preferred_element_type=F32)
            for blk in range(n_blocks):
                res[blk].append((us[blk], mixed[:, blk * GM_DH:(blk + 1) * GM_DH] + bsp_ref[hd]))
        return res

    def gm_gate(res):
        outs = []
        for hd, (u, mixed) in enumerate(res):
            z = _rms(u * mixed, RMS_EPS) * og_ref[:, hd * GM_DH:(hd + 1) * GM_DH]
            outs.append(z.astype(BF16))
        return jnp.concatenate(outs, axis=1)

    def hg_gates(r0):
        q = proj(r0, CHUNK, 0, HG_WIDTH)
        f_raw = proj(r0, CHUNK, HG_WIDTH, 2 * HG_WIDTH)
        vb = proj(r0, CHUNK, 2 * HG_WIDTH, 3 * HG_WIDTH).astype(BF16)
        g = proj(r0, CHUNK, 3 * HG_WIDTH, 4 * HG_WIDTH)
        g_act = g * jax.nn.sigmoid(g)
        f = lb + (1.0 - lb) * jax.nn.sigmoid(f_raw)
        log_f = jnp.log(f) * LOG2_E
        k = 1.0 - f
        hi = log_f.astype(BF16)
        mid = (log_f - hi.astype(F32)).astype(BF16)
        b = jnp.dot(prefix_ref[...], jnp.concatenate([hi, mid], axis=0),
                    preferred_element_type=F32)
        return q, k, vb, g_act, b

    def hg_scores(gates):
        q, k, vb_all, g_act, b = gates
        decay_in = jnp.exp2(b)
        decay_out = jnp.exp2(b[CHUNK - 1:CHUNK, :] - b)
        lvl = [(_select_rows(m, q, k) * jnp.exp2(_level_log_decay(b, m))).astype(BF16)
               for m in HALF_SIZES]
        qb = q.astype(BF16)
        kb = k.astype(BF16)
        q_in = (q * decay_in).astype(BF16)
        k_out = (k * decay_out).astype(BF16)
        dn = (((1,), (1,)), ((), ()))
        per_head = []
        for hd in range(HG_HEADS):
            cs = slice(hd * HG_DK, (hd + 1) * HG_DK)
            att = lax.dot_general(qb[:, cs], kb[:, cs], dn, preferred_element_type=F32) * pair_ref[N_LEVELS]
            for i in range(N_LEVELS):
                mh = lvl[i][:, cs]
                att = att + lax.dot_general(mh, mh, dn, preferred_element_type=F32) * pair_ref[i]
            vb = vb_all[:, cs]
            inc = lax.dot_general(vb, k_out[:, cs], (((0,), (0,)), ((), ())),
                                  preferred_element_type=F32)
            per_head.append((att.astype(BF16), vb, q_in[:, cs], inc, decay_in[CHUNK - 1:CHUNK, cs],
                             g_act[:, cs]))
        return per_head

    def hg_output(per_head):
        dn = (((1,), (1,)), ((), ()))
        outs = []
        for hd, (att, vb, q_in, inc, decay_last, g_act) in enumerate(per_head):
            cs = slice(hd * HG_DK, (hd + 1) * HG_DK)
            st = st_ref[hd]
            o = (jnp.dot(att, vb, preferred_element_type=F32)
                 + lax.dot_general(q_in, st.astype(BF16), dn, preferred_element_type=F32))
            st_ref[hd] = st * decay_last + inc
            o = _rms(o, RMS_EPS) * gn_ref[:, cs] * g_act
            outs.append(o.astype(BF16))
        return jnp.concatenate(outs, axis=1)

    n_sub = MIX_ROWS // SUB_ROWS
    n_chunks = MIX_ROWS // CHUNK
    chunks_per_sub = SUB_ROWS // CHUNK
    chunks_per_blk = GM_BLOCK // CHUNK
    blocks_per_sub = SUB_ROWS // GM_BLOCK
    in_piece = IN_COLS // chunks_per_sub
    out_piece = D_MODEL // chunks_per_sub
    lag_out = 2

    y = [normalize(s * SUB_ROWS) for s in range(n_sub)]
    project(0, y[0], 0, IN_COLS)
    gates, scores, gm_state, hg_out, gm_out, cat = {}, {}, {}, {}, {}, {}
    for slot in range(n_chunks + lag_out + chunks_per_sub):
        s_next = slot // chunks_per_sub + 1
        if s_next < n_sub:
            p = slot % chunks_per_sub
            project(s_next * SUB_ROWS, y[s_next], p * in_piece, (p + 1) * in_piece)
        done = slot - lag_out - chunks_per_sub
        if done >= 0 and done // chunks_per_sub < n_sub:
            s_done, p = divmod(done, chunks_per_sub)
            if p == 0:
                c0 = s_done * chunks_per_sub
                hg_part = jnp.concatenate([hg_out.pop(c0 + i) for i in range(chunks_per_sub)], axis=0)
                gm_part = jnp.concatenate(
                    [gm_out.pop(c0 + i) for i in range(0, chunks_per_sub, chunks_per_blk)], axis=0)
                cat[s_done] = jnp.concatenate([hg_part, gm_part], axis=1)
            out_project(s_done * SUB_ROWS, cat[s_done], p * out_piece, (p + 1) * out_piece)
        if slot < n_chunks:
            gates[slot] = hg_gates(slot * CHUNK)
            p = slot % chunks_per_sub
            if p == 0:
                for i, res in enumerate(gm_mix(slot * CHUNK, blocks_per_sub)):
                    gm_state[slot + i * chunks_per_blk] = res
            elif p <= blocks_per_sub:
                key = slot - p + (p - 1) * chunks_per_blk
                gm_out[key] = gm_gate(gm_state.pop(key))
        if 0 <= slot - 1 < n_chunks:
            scores[slot - 1] = hg_scores(gates.pop(slot - 1))
        if 0 <= slot - lag_out < n_chunks:
            hg_out[slot - lag_out] = hg_output(scores.pop(slot - lag_out))


def _mixer_call(h, mod, ngain, w_in, w_out, lb_logits, gnorm, ln_gain, out_gain, w_sp, b_sp,
                prefix_c, pair_c, *, layer):
    batch = h.shape[0]
    depth = lb_logits.shape[0]
    kern = functools.partial(_mixer_kernel, layer=layer)
    return pl.pallas_call(
        kern,
        grid=(batch, SEQ // MIX_ROWS),
        in_specs=[
            pl.BlockSpec((1, MIX_ROWS, D_MODEL), lambda b, j: (b, j, 0)),
            pl.BlockSpec((1, N_MOD, D_MODEL), lambda b, j: (b, 0, 0)),
            _resident((1, D_MODEL)),
            pl.BlockSpec(memory_space=pl.ANY),
            pl.BlockSpec(memory_space=pl.ANY),
            _resident((depth, HG_WIDTH)),
            _resident((1, HG_WIDTH)),
            _resident((1, GM_WIDTH)),
            _resident((1, GM_WIDTH)),
            _resident((GM_HEADS, GM_BLOCK, GM_BLOCK)),
            _resident((GM_HEADS, GM_BLOCK, 1)),
            _resident(prefix_c.shape),
            _resident(pair_c.shape),
        ],
        out_specs=pl.BlockSpec((1, MIX_ROWS, D_MODEL), lambda b, j: (b, j, 0)),
        out_shape=jax.ShapeDtypeStruct(h.shape, F32),
        scratch_shapes=[pltpu.VMEM((HG_HEADS, HG_DK, HG_DK), F32),
                        pltpu.VMEM((MIX_ROWS, IN_COLS), F32),
                        pltpu.VMEM((IN_COLS // WIN_BLOCK, D_MODEL, WIN_BLOCK), BF16),
                        pltpu.VMEM((D_MODEL // WOUT_BLOCK, D_MODEL, WOUT_BLOCK), BF16),
                        pltpu.VMEM((2, D_MODEL, WIN_BLOCK), F32),
                        pltpu.VMEM((2, D_MODEL, WOUT_BLOCK), F32),
                        pltpu.SemaphoreType.DMA((2,)),
                        pltpu.SemaphoreType.DMA((2,))],
        compiler_params=pltpu.CompilerParams(
            dimension_semantics=("arbitrary", "arbitrary"), vmem_limit_bytes=VMEM_LIMIT),
        name="mixer",
    )(h, mod, ngain, w_in, w_out, lb_logits, gnorm, ln_gain, out_gain, w_sp, b_sp, prefix_c, pair_c)


def kernel(x, c, w_ada, b_ada, norm_gain, ffn1_w13, ffn1_w2, w_in, hg_lb_logits, hg_gnorm,
           gm_ln_gain, gm_w_spatial, gm_b_spatial, gm_out_gain, w_out, ffn2_w13, ffn2_w2,
           final_gain):
    batch, seq, _ = x.shape
    depth = w_ada.shape[0]
    prefix_np, pair_np = _hgrn_constants()
    prefix_c = jnp.asarray(prefix_np, BF16)
    pair_c = jnp.asarray(pair_np, F32)
    fgain = final_gain.reshape(1, D_MODEL)

    mod_all = _ada_call(c, w_ada, b_ada).reshape(depth, batch, N_MOD, D_MODEL)
    h = x
    for l in range(depth):
        mod = mod_all[l]
        h = _ffn_call(h.reshape(batch * seq, D_MODEL), mod, norm_gain[l, 0:1], ffn1_w13, ffn1_w2,
                      fgain, layer=l, mod_base=0, final=False).reshape(batch, seq, D_MODEL)
        h = _mixer_call(h, mod, norm_gain[l, 1:2], w_in, w_out,
                        hg_lb_logits, hg_gnorm[l:l + 1], gm_ln_gain[l:l + 1], gm_out_gain[l:l + 1],
                        gm_w_spatial[l], gm_b_spatial[l].reshape(GM_HEADS, GM_BLOCK, 1),
                        prefix_c, pair_c, layer=l)
        h = _ffn_call(h.reshape(batch * seq, D_MODEL), mod, norm_gain[l, 2:3], ffn2_w13, ffn2_w2,
                      fgain, layer=l, mod_base=6,
                      final=(l == depth - 1)).reshape(batch, seq, D_MODEL)
    return h
```

```python
import functools

import numpy as np
import jax
import jax.numpy as jnp
from jax import lax
from jax.experimental import pallas as pl
from jax.experimental.pallas import tpu as pltpu

D_MODEL = 1024
SEQ = 2048
CHUNK = 64
HG_WIDTH = 512
HG_HEADS = 4
HG_DK = 128
GM_WIDTH = 512
GM_HEADS = 4
GM_DH = 128
GM_BLOCK = 128
D_FF = 2816
N_MOD = 9
IN_COLS = 4 * HG_WIDTH + 2 * GM_WIDTH
RMS_EPS = 1e-6
LN_EPS = 1e-5

FF_CHUNK = 256
N_FF_CHUNKS = D_FF // FF_CHUNK
W13_BLOCK = 2 * FF_CHUNK
WIN_BLOCK = 768
WOUT_BLOCK = 256
FFN_ROWS = 512
MIX_ROWS = 512
SUB_ROWS = 256
ADA_COLS = 1024
HALF_SIZES = (32, 16, 8, 4, 2, 1)
N_LEVELS = len(HALF_SIZES)
MATMUL_SIZES = HALF_SIZES[:-1]
SUBLANES = 8
VMEM_LIMIT = 56 * 1024 * 1024
LOG2_E = 1.4426950408889634

F32 = jnp.float32
BF16 = jnp.bfloat16


def _rms(x, eps):
    return x * lax.rsqrt(jnp.mean(x * x, axis=-1, keepdims=True) + eps)


def _resident(shape):
    return pl.BlockSpec(shape, lambda *_: (0,) * len(shape), pipeline_mode=pl.Buffered(1))


def _ada_kernel(c_ref, w_ref, b_ref, o_ref):
    c = c_ref[...]
    ca = c * jax.nn.sigmoid(c)
    batch = ca.shape[0]
    ca_hi = ca.astype(BF16)
    ca_lo = (ca - ca_hi.astype(F32)).astype(BF16)
    w = w_ref[0]
    w_hi = w.astype(BF16)
    w_lo = (w - w_hi.astype(F32)).astype(BF16)
    p = jnp.dot(jnp.concatenate([ca_hi, ca_lo], axis=0), w_hi, preferred_element_type=F32)
    q = jnp.dot(ca_hi, w_lo, preferred_element_type=F32)
    o_ref[0] = p[:batch] + p[batch:] + q + b_ref[0]


def _ada_call(c, w_ada, b_ada):
    depth, _, n_out = w_ada.shape
    batch = c.shape[0]
    return pl.pallas_call(
        _ada_kernel,
        grid=(depth, n_out // ADA_COLS),
        in_specs=[
            pl.BlockSpec((batch, D_MODEL), lambda l, j: (0, 0)),
            pl.BlockSpec((1, D_MODEL, ADA_COLS), lambda l, j: (l, 0, j)),
            pl.BlockSpec((1, 1, ADA_COLS), lambda l, j: (l, 0, j)),
        ],
        out_specs=pl.BlockSpec((1, batch, ADA_COLS), lambda l, j: (l, 0, j)),
        out_shape=jax.ShapeDtypeStruct((depth, batch, n_out), F32),
        compiler_params=pltpu.CompilerParams(
            dimension_semantics=("arbitrary", "arbitrary"), vmem_limit_bytes=VMEM_LIMIT),
        name="ada",
    )(c, w_ada, b_ada.reshape(depth, 1, n_out))


def _stream_cast(n_blocks, src_block, dst_ref, stage_ref, sem_ref):
    def copy(i, slot):
        return pltpu.make_async_copy(src_block(i), stage_ref.at[slot], sem_ref.at[slot])

    copy(0, 0).start()

    def body(i, carry):
        slot = i % 2

        @pl.when(i + 1 < n_blocks)
        def _():
            copy(i + 1, 1 - slot).start()

        copy(i, slot).wait()
        dst_ref[i] = stage_ref[slot].astype(BF16)
        return carry

    lax.fori_loop(0, n_blocks, body, 0)


def _col_block(w_hbm, layer, width):
    return lambda i: w_hbm.at[layer, :, pl.ds(pl.multiple_of(i * width, width), width)]


def _row_block(w_hbm, layer, height):
    return lambda i: w_hbm.at[layer, pl.ds(pl.multiple_of(i * height, height), height), :]


def _cols(w_ref, c0, width):
    block_width = w_ref.shape[2]
    blk, off = divmod(c0, block_width)
    assert off + width <= block_width
    return w_ref[blk, :, off:off + width]


def _ffn_kernel(h_ref, mod_ref, ng_ref, w13_hbm, w2_hbm, fg_ref, o_ref, w13_ref, w2_ref,
                stage13_ref, stage2_ref, sem13_ref, sem2_ref, *, layer, mod_base, final):
    @pl.when(pl.program_id(0) == 0)
    def _():
        _stream_cast(w13_ref.shape[0], _col_block(w13_hbm, layer, W13_BLOCK), w13_ref,
                     stage13_ref, sem13_ref)
        _stream_cast(w2_ref.shape[0], _row_block(w2_hbm, layer, FF_CHUNK), w2_ref,
                     stage2_ref, sem2_ref)

    h = h_ref[...]
    sh = mod_ref[0, mod_base:mod_base + 1, :]
    sc = mod_ref[0, mod_base + 1:mod_base + 2, :]
    gate = mod_ref[0, mod_base + 2:mod_base + 3, :]
    y = _rms(h, RMS_EPS) * ng_ref[...]
    yb = (y * (1.0 + sc) + sh).astype(BF16)
    acc = None
    for j in range(N_FF_CHUNKS):
        c0 = j * FF_CHUNK
        a = jnp.dot(yb, _cols(w13_ref, c0, FF_CHUNK), preferred_element_type=F32)
        b = jnp.dot(yb, _cols(w13_ref, D_FF + c0, FF_CHUNK), preferred_element_type=F32)
        hm = (a * jax.nn.sigmoid(a) * b).astype(BF16)
        d = jnp.dot(hm, w2_ref[j], preferred_element_type=F32)
        acc = d if acc is None else acc + d
    res = h_ref[...] + 0.5 * gate * acc
    if final:
        res = _rms(res, RMS_EPS) * fg_ref[...]
    o_ref[...] = res


def _ffn_call(h2d, mod, ngain, w13, w2, fgain, *, layer, mod_base, final):
    n_tok = h2d.shape[0]
    tiles_per_seq = SEQ // FFN_ROWS
    kern = functools.partial(_ffn_kernel, layer=layer, mod_base=mod_base, final=final)
    return pl.pallas_call(
        kern,
        grid=(n_tok // FFN_ROWS,),
        in_specs=[
            pl.BlockSpec((FFN_ROWS, D_MODEL), lambda i: (i, 0)),
            pl.BlockSpec((1, N_MOD, D_MODEL), lambda i: (i // tiles_per_seq, 0, 0)),
            _resident((1, D_MODEL)),
            pl.BlockSpec(memory_space=pl.ANY),
            pl.BlockSpec(memory_space=pl.ANY),
            _resident((1, D_MODEL)),
        ],
        out_specs=pl.BlockSpec((FFN_ROWS, D_MODEL), lambda i: (i, 0)),
        out_shape=jax.ShapeDtypeStruct(h2d.shape, F32),
        scratch_shapes=[
            pltpu.VMEM((2 * D_FF // W13_BLOCK, D_MODEL, W13_BLOCK), BF16),
            pltpu.VMEM((N_FF_CHUNKS, FF_CHUNK, D_MODEL), BF16),
            pltpu.VMEM((2, D_MODEL, W13_BLOCK), F32),
            pltpu.VMEM((2, FF_CHUNK, D_MODEL), F32),
            pltpu.SemaphoreType.DMA((2,)),
            pltpu.SemaphoreType.DMA((2,)),
        ],
        compiler_params=pltpu.CompilerParams(
            dimension_semantics=("arbitrary",), vmem_limit_bytes=VMEM_LIMIT),
        name="ffn_final" if final else "ffn",
    )(h2d, mod, ngain, w13, w2, fgain)


def _hgrn_constants():
    r = np.arange(CHUNK)
    pair = np.zeros((N_LEVELS + 1, CHUNK, CHUNK), np.float32)
    prefix = (r[None, :] <= r[:, None]).astype(np.float32)
    for i, m in enumerate(HALF_SIZES):
        blk = r // (2 * m)
        low = (r % (2 * m)) < m
        pair[i] = (blk[:, None] == blk[None, :]) & (~low)[:, None] & low[None, :]
    pair[N_LEVELS] = np.eye(CHUNK)
    return np.concatenate([prefix, prefix], axis=1), pair


def _level_log_decay(b, m):
    rows, width = b.shape
    if m >= SUBLANES:
        parts = []
        for r0 in range(0, rows, 2 * m):
            bref = b[r0 + m - 1:r0 + m, :]
            parts.append(bref - b[r0:r0 + m])
            parts.append(b[r0 + m:r0 + 2 * m] - bref)
        return jnp.concatenate(parts, axis=0)
    row = lax.broadcasted_iota(jnp.int32, b.shape, 0)
    if m == 1:
        bref = pltpu.roll(b, 1, 0)
    else:
        def group_row(i):
            return jnp.concatenate(
                [jnp.broadcast_to(b[g + i:g + i + 1, :], (SUBLANES, width))
                 for g in range(0, rows, SUBLANES)], axis=0)
        bref = group_row(m - 1)
        for blk in range(1, SUBLANES // (2 * m)):
            bref = jnp.where((row % SUBLANES) >= blk * 2 * m, group_row(blk * 2 * m + m - 1), bref)
    upper = (row & m) != 0
    if m == 1:
        return jnp.where(upper, b - bref, 0.0)
    return jnp.where(upper, b - bref, bref - b)


def _select_rows(m, upper_val, lower_val):
    rows = upper_val.shape[0]
    if m >= SUBLANES:
        parts = []
        for r0 in range(0, rows, m):
            src = upper_val if (r0 // m) % 2 == 1 else lower_val
            parts.append(src[r0:r0 + m])
        return jnp.concatenate(parts, axis=0)
    row = lax.broadcasted_iota(jnp.int32, upper_val.shape, 0)
    return jnp.where((row & m) != 0, upper_val, lower_val)


def _mixer_kernel(h_ref, mod_ref, ng_ref, win_hbm, wout_hbm, lbl_ref, gn_ref, lng_ref, og_ref,
                  wsp_ref, bsp_ref, prefix_ref, pair_ref, o_ref, st_ref, proj_ref, win_ref, wout_ref,
                  stage_in_ref, stage_out_ref, sem_in_ref, sem_out_ref, *, layer):
    @pl.when((pl.program_id(0) == 0) & (pl.program_id(1) == 0))
    def _():
        _stream_cast(win_ref.shape[0], _col_block(win_hbm, layer, WIN_BLOCK), win_ref,
                     stage_in_ref, sem_in_ref)
        _stream_cast(wout_ref.shape[0], _col_block(wout_hbm, layer, WOUT_BLOCK), wout_ref,
                     stage_out_ref, sem_out_ref)

    @pl.when(pl.program_id(1) == 0)
    def _():
        st_ref[...] = jnp.zeros_like(st_ref)

    sh = mod_ref[0, 3:4, :]
    sc = mod_ref[0, 4:5, :]
    gate = mod_ref[0, 5:6, :]

    def normalize(r0):
        y = _rms(h_ref[0, r0:r0 + SUB_ROWS, :], RMS_EPS) * ng_ref[...]
        return (y * (1.0 + sc) + sh).astype(BF16)

    def proj(r0, rows, c0, c1):
        return proj_ref[r0:r0 + rows, c0:c1]

    def project(r0, y, c0, c1):
        for c in range(c0, c1, WIN_BLOCK):
            proj_ref[r0:r0 + SUB_ROWS, c:c + WIN_BLOCK] = jnp.dot(
                y, _cols(win_ref, c, WIN_BLOCK), preferred_element_type=F32)

    def out_project(r0, cat, c0, c1):
        mix = jnp.dot(cat, _cols(wout_ref, c0, c1 - c0), preferred_element_type=F32)
        o_ref[0, r0:r0 + SUB_ROWS, c0:c1] = h_ref[0, r0:r0 + SUB_ROWS, c0:c1] + gate[:, c0:c1] * mix

    lg = lbl_ref[...]
    ex = jnp.exp(lg - jnp.max(lg, axis=0, keepdims=True))
    prob = ex / jnp.sum(ex, axis=0, keepdims=True)
    lb = jnp.zeros((1, HG_WIDTH), F32)
    for i in range(1, layer + 1):
        lb = lb + prob[i:i + 1, :]

    r2 = lax.broadcasted_iota(jnp.int32, (GM_BLOCK, GM_BLOCK), 0)
    c2 = lax.broadcasted_iota(jnp.int32, (GM_BLOCK, GM_BLOCK), 1)
    causal = (r2 // CHUNK) >= (c2 // CHUNK)
    w_sp = [jnp.where(causal, wsp_ref[hd], 0.0).astype(BF16) for hd in range(GM_HEADS)]

    def gm_mix(r0, n_blocks):
        res = [[] for _ in range(n_blocks)]
        for hd in range(GM_HEADS):
            cu = 4 * HG_WIDTH + hd * GM_DH
            cv = cu + GM_WIDTH
            us, vns = [], []
            for blk in range(n_blocks):
                rb = r0 + blk * GM_BLOCK
                us.append(jax.nn.gelu(proj(rb, GM_BLOCK, cu, cu + GM_DH)))
                v = jax.nn.gelu(proj(rb, GM_BLOCK, cv, cv + GM_DH))
                vc = v - jnp.mean(v, axis=-1, keepdims=True)
                var = jnp.mean(vc * vc, axis=-1, keepdims=True)
                vn = vc * lax.rsqrt(var + LN_EPS) * lng_ref[:, hd * GM_DH:(hd + 1) * GM_DH]
                vns.append(vn.astype(BF16))
            mixed = jnp.dot(w_sp[hd], jnp.concatenate(vns, axis=1), preferred_element_type=F32)
            for blk in range(n_blocks):
                res[blk].append((us[blk], mixed[:, blk * GM_DH:(blk + 1) * GM_DH] + bsp_ref[hd]))
        return res

    def gm_gate(res):
        outs = []
        for hd, (u, mixed) in enumerate(res):
            z = _rms(u * mixed, RMS_EPS) * og_ref[:, hd * GM_DH:(hd + 1) * GM_DH]
            outs.append(z.astype(BF16))
        return jnp.concatenate(outs, axis=1)

    def hg_gates(r0):
        q = proj(r0, CHUNK, 0, HG_WIDTH)
        f_raw = proj(r0, CHUNK, HG_WIDTH, 2 * HG_WIDTH)
        vb = proj(r0, CHUNK, 2 * HG_WIDTH, 3 * HG_WIDTH).astype(BF16)
        g = proj(r0, CHUNK, 3 * HG_WIDTH, 4 * HG_WIDTH)
        g_act = g * jax.nn.sigmoid(g)
        f = lb + (1.0 - lb) * jax.nn.sigmoid(f_raw)
        log_f = jnp.log(f) * LOG2_E
        k = 1.0 - f
        hi = log_f.astype(BF16)
        mid = (log_f - hi.astype(F32)).astype(BF16)
        b = jnp.dot(prefix_ref[...], jnp.concatenate([hi, mid], axis=0),
                    preferred_element_type=F32)
        return q, k, f, vb, g_act, b

    def hg_scores(gates):
        q, k, f, vb_all, g_act, b = gates
        decay_in = jnp.exp2(b)
        decay_out = jnp.exp2(b[CHUNK - 1:CHUNK, :] - b)
        lvl = [(_select_rows(m, q, k) * jnp.exp2(_level_log_decay(b, m))).astype(BF16)
               for m in MATMUL_SIZES]
        q_in = (q * decay_in).astype(BF16)
        k_out = (k * decay_out).astype(BF16)
        same = q * k
        prev = q * pltpu.roll(k, 1, 0) * f
        dn = (((1,), (1,)), ((), ()))
        per_head = []
        for hd in range(HG_HEADS):
            cs = slice(hd * HG_DK, (hd + 1) * HG_DK)
            att = (jnp.sum(same[:, cs], axis=-1, keepdims=True) * pair_ref[N_LEVELS]
                   + jnp.sum(prev[:, cs], axis=-1, keepdims=True) * pair_ref[N_LEVELS - 1])
            for i in range(len(MATMUL_SIZES)):
                mh = lvl[i][:, cs]
                att = att + lax.dot_general(mh, mh, dn, preferred_element_type=F32) * pair_ref[i]
            vb = vb_all[:, cs]
            inc = lax.dot_general(vb, k_out[:, cs], (((0,), (0,)), ((), ())),
                                  preferred_element_type=F32)
            per_head.append((att.astype(BF16), vb, q_in[:, cs], inc, decay_in[CHUNK - 1:CHUNK, cs],
                             g_act[:, cs]))
        return per_head

    def hg_output(per_head):
        dn = (((1,), (1,)), ((), ()))
        outs = []
        for hd, (att, vb, q_in, inc, decay_last, g_act) in enumerate(per_head):
            cs = slice(hd * HG_DK, (hd + 1) * HG_DK)
            st = st_ref[hd]
            o = (jnp.dot(att, vb, preferred_element_type=F32)
                 + lax.dot_general(q_in, st.astype(BF16), dn, preferred_element_type=F32))
            st_ref[hd] = st * decay_last + inc
            o = _rms(o, RMS_EPS) * gn_ref[:, cs] * g_act
            outs.append(o.astype(BF16))
        return jnp.concatenate(outs, axis=1)

    n_sub = MIX_ROWS // SUB_ROWS
    n_chunks = MIX_ROWS // CHUNK
    chunks_per_sub = SUB_ROWS // CHUNK
    chunks_per_blk = GM_BLOCK // CHUNK
    blocks_per_sub = SUB_ROWS // GM_BLOCK
    in_piece = IN_COLS // chunks_per_sub
    out_piece = D_MODEL // chunks_per_sub
    lag_out = 2

    y = [normalize(s * SUB_ROWS) for s in range(n_sub)]
    project(0, y[0], 0, IN_COLS)
    gates, scores, gm_state, hg_out, gm_out, cat = {}, {}, {}, {}, {}, {}
    for slot in range(n_chunks + lag_out + chunks_per_sub):
        s_next = slot // chunks_per_sub + 1
        if s_next < n_sub:
            p = slot % chunks_per_sub
            project(s_next * SUB_ROWS, y[s_next], p * in_piece, (p + 1) * in_piece)
        done = slot - lag_out - chunks_per_sub
        if done >= 0 and done // chunks_per_sub < n_sub:
            s_done, p = divmod(done, chunks_per_sub)
            if p == 0:
                c0 = s_done * chunks_per_sub
                hg_part = jnp.concatenate([hg_out.pop(c0 + i) for i in range(chunks_per_sub)], axis=0)
                gm_part = jnp.concatenate(
                    [gm_out.pop(c0 + i) for i in range(0, chunks_per_sub, chunks_per_blk)], axis=0)
                cat[s_done] = jnp.concatenate([hg_part, gm_part], axis=1)
            out_project(s_done * SUB_ROWS, cat[s_done], p * out_piece, (p + 1) * out_piece)
        if slot < n_chunks:
            gates[slot] = hg_gates(slot * CHUNK)
            p = slot % chunks_per_sub
            if p == 0:
                for i, res in enumerate(gm_mix(slot * CHUNK, blocks_per_sub)):
                    gm_state[slot + i * chunks_per_blk] = res
            elif p <= blocks_per_sub:
                key = slot - p + (p - 1) * chunks_per_blk
                gm_out[key] = gm_gate(gm_state.pop(key))
        if 0 <= slot - 1 < n_chunks:
            scores[slot - 1] = hg_scores(gates.pop(slot - 1))
        if 0 <= slot - lag_out < n_chunks:
            hg_out[slot - lag_out] = hg_output(scores.pop(slot - lag_out))


def _mixer_call(h, mod, ngain, w_in, w_out, lb_logits, gnorm, ln_gain, out_gain, w_sp, b_sp,
                prefix_c, pair_c, *, layer):
    batch = h.shape[0]
    depth = lb_logits.shape[0]
    kern = functools.partial(_mixer_kernel, layer=layer)
    return pl.pallas_call(
        kern,
        grid=(batch, SEQ // MIX_ROWS),
        in_specs=[
            pl.BlockSpec((1, MIX_ROWS, D_MODEL), lambda b, j: (b, j, 0)),
            pl.BlockSpec((1, N_MOD, D_MODEL), lambda b, j: (b, 0, 0)),
            _resident((1, D_MODEL)),
            pl.BlockSpec(memory_space=pl.ANY),
            pl.BlockSpec(memory_space=pl.ANY),
            _resident((depth, HG_WIDTH)),
            _resident((1, HG_WIDTH)),
            _resident((1, GM_WIDTH)),
            _resident((1, GM_WIDTH)),
            _resident((GM_HEADS, GM_BLOCK, GM_BLOCK)),
            _resident((GM_HEADS, GM_BLOCK, 1)),
            _resident(prefix_c.shape),
            _resident(pair_c.shape),
        ],
        out_specs=pl.BlockSpec((1, MIX_ROWS, D_MODEL), lambda b, j: (b, j, 0)),
        out_shape=jax.ShapeDtypeStruct(h.shape, F32),
        scratch_shapes=[pltpu.VMEM((HG_HEADS, HG_DK, HG_DK), F32),
                        pltpu.VMEM((MIX_ROWS, IN_COLS), F32),
                        pltpu.VMEM((IN_COLS // WIN_BLOCK, D_MODEL, WIN_BLOCK), BF16),
                        pltpu.VMEM((D_MODEL // WOUT_BLOCK, D_MODEL, WOUT_BLOCK), BF16),
                        pltpu.VMEM((2, D_MODEL, WIN_BLOCK), F32),
                        pltpu.VMEM((2, D_MODEL, WOUT_BLOCK), F32),
                        pltpu.SemaphoreType.DMA((2,)),
                        pltpu.SemaphoreType.DMA((2,))],
        compiler_params=pltpu.CompilerParams(
            dimension_semantics=("arbitrary", "arbitrary"), vmem_limit_bytes=VMEM_LIMIT),
        name="mixer",
    )(h, mod, ngain, w_in, w_out, lb_logits, gnorm, ln_gain, out_gain, w_sp, b_sp, prefix_c, pair_c)


def kernel(x, c, w_ada, b_ada, norm_gain, ffn1_w13, ffn1_w2, w_in, hg_lb_logits, hg_gnorm,
           gm_ln_gain, gm_w_spatial, gm_b_spatial, gm_out_gain, w_out, ffn2_w13, ffn2_w2,
           final_gain):
    batch, seq, _ = x.shape
    depth = w_ada.shape[0]
    prefix_np, pair_np = _hgrn_constants()
    prefix_c = jnp.asarray(prefix_np, BF16)
    pair_c = jnp.asarray(pair_np, F32)
    fgain = final_gain.reshape(1, D_MODEL)

    mod_all = _ada_call(c, w_ada, b_ada).reshape(depth, batch, N_MOD, D_MODEL)
    h = x
    for l in range(depth):
        mod = mod_all[l]
        h = _ffn_call(h.reshape(batch * seq, D_MODEL), mod, norm_gain[l, 0:1], ffn1_w13, ffn1_w2,
                      fgain, layer=l, mod_base=0, final=False).reshape(batch, seq, D_MODEL)
        h = _mixer_call(h, mod, norm_gain[l, 1:2], w_in, w_out,
                        hg_lb_logits, hg_gnorm[l:l + 1], gm_ln_gain[l:l + 1], gm_out_gain[l:l + 1],
                        gm_w_spatial[l], gm_b_spatial[l].reshape(GM_HEADS, GM_BLOCK, 1),
                        prefix_c, pair_c, layer=l)
        h = _ffn_call(h.reshape(batch * seq, D_MODEL), mod, norm_gain[l, 2:3], ffn2_w13, ffn2_w2,
                      fgain, layer=l, mod_base=6,
                      final=(l == depth - 1)).reshape(batch, seq, D_MODEL)
    return h
```

```python
import functools

import numpy as np
import jax
import jax.numpy as jnp
from jax import lax
from jax.experimental import pallas as pl
from jax.experimental.pallas import tpu as pltpu

D_MODEL = 1024
SEQ = 2048
CHUNK = 64
HG_WIDTH = 512
HG_HEADS = 4
HG_DK = 128
GM_WIDTH = 512
GM_HEADS = 4
GM_DH = 128
GM_BLOCK = 128
D_FF = 2816
N_MOD = 9
IN_COLS = 4 * HG_WIDTH + 2 * GM_WIDTH
RMS_EPS = 1e-6
LN_EPS = 1e-5

FF_CHUNK = 256
N_FF_CHUNKS = D_FF // FF_CHUNK
W13_BLOCK = 2 * FF_CHUNK
WIN_BLOCK = 768
WOUT_BLOCK = 256
FFN_ROWS = 512
MIX_ROWS = 512
SUB_ROWS = 256
ADA_COLS = 2304
HALF_SIZES = (32, 16, 8, 4, 2, 1)
N_LEVELS = len(HALF_SIZES)
MATMUL_SIZES = HALF_SIZES[:-1]
SUBLANES = 8
VMEM_LIMIT = 56 * 1024 * 1024
LOG2_E = 1.4426950408889634

F32 = jnp.float32
BF16 = jnp.bfloat16


def _rms(x, eps):
    return x * lax.rsqrt(jnp.mean(x * x, axis=-1, keepdims=True) + eps)


def _resident(shape):
    return pl.BlockSpec(shape, lambda *_: (0,) * len(shape), pipeline_mode=pl.Buffered(1))


def _ada_kernel(c_ref, w_ref, b_ref, o_ref):
    c = c_ref[...]
    ca = c * jax.nn.sigmoid(c)
    batch = ca.shape[0]
    ca_hi = ca.astype(BF16)
    ca_lo = (ca - ca_hi.astype(F32)).astype(BF16)
    w = w_ref[0]
    w_hi = w.astype(BF16)
    w_lo = (w - w_hi.astype(F32)).astype(BF16)
    p = jnp.dot(jnp.concatenate([ca_hi, ca_lo], axis=0), w_hi, preferred_element_type=F32)
    q = jnp.dot(ca_hi, w_lo, preferred_element_type=F32)
    o_ref[0] = p[:batch] + p[batch:] + q + b_ref[0]


def _ada_call(c, w_ada, b_ada):
    depth, _, n_out = w_ada.shape
    batch = c.shape[0]
    return pl.pallas_call(
        _ada_kernel,
        grid=(depth, n_out // ADA_COLS),
        in_specs=[
            pl.BlockSpec((batch, D_MODEL), lambda l, j: (0, 0)),
            pl.BlockSpec((1, D_MODEL, ADA_COLS), lambda l, j: (l, 0, j)),
            pl.BlockSpec((1, 1, ADA_COLS), lambda l, j: (l, 0, j)),
        ],
        out_specs=pl.BlockSpec((1, batch, ADA_COLS), lambda l, j: (l, 0, j)),
        out_shape=jax.ShapeDtypeStruct((depth, batch, n_out), F32),
        compiler_params=pltpu.CompilerParams(
            dimension_semantics=("arbitrary", "arbitrary"), vmem_limit_bytes=VMEM_LIMIT),
        name="ada",
    )(c, w_ada, b_ada.reshape(depth, 1, n_out))


def _stream_cast(n_blocks, src_block, dst_ref, stage_ref, sem_ref):
    def copy(i, slot):
        return pltpu.make_async_copy(src_block(i), stage_ref.at[slot], sem_ref.at[slot])

    copy(0, 0).start()

    def body(i, carry):
        slot = i % 2

        @pl.when(i + 1 < n_blocks)
        def _():
            copy(i + 1, 1 - slot).start()

        copy(i, slot).wait()
        dst_ref[i] = stage_ref[slot].astype(BF16)
        return carry

    lax.fori_loop(0, n_blocks, body, 0)


def _col_block(w_hbm, layer, width):
    return lambda i: w_hbm.at[layer, :, pl.ds(pl.multiple_of(i * width, width), width)]


def _row_block(w_hbm, layer, height):
    return lambda i: w_hbm.at[layer, pl.ds(pl.multiple_of(i * height, height), height), :]


def _cols(w_ref, c0, width):
    block_width = w_ref.shape[2]
    blk, off = divmod(c0, block_width)
    assert off + width <= block_width
    return w_ref[blk, :, off:off + width]


def _ffn_kernel(h_ref, mod_ref, ng_ref, w13_hbm, w2_hbm, fg_ref, o_ref, w13_ref, w2_ref,
                stage13_ref, stage2_ref, sem13_ref, sem2_ref, *, layer, mod_base, final):
    @pl.when(pl.program_id(0) == 0)
    def _():
        _stream_cast(w13_ref.shape[0], _col_block(w13_hbm, layer, W13_BLOCK), w13_ref,
                     stage13_ref, sem13_ref)
        _stream_cast(w2_ref.shape[0], _row_block(w2_hbm, layer, FF_CHUNK), w2_ref,
                     stage2_ref, sem2_ref)

    h = h_ref[...]
    sh = mod_ref[0, mod_base:mod_base + 1, :]
    sc = mod_ref[0, mod_base + 1:mod_base + 2, :]
    gate = mod_ref[0, mod_base + 2:mod_base + 3, :]
    y = _rms(h, RMS_EPS) * ng_ref[...]
    yb = (y * (1.0 + sc) + sh).astype(BF16)
    acc = None
    for j in range(N_FF_CHUNKS):
        c0 = j * FF_CHUNK
        a = jnp.dot(yb, _cols(w13_ref, c0, FF_CHUNK), preferred_element_type=F32)
        b = jnp.dot(yb, _cols(w13_ref, D_FF + c0, FF_CHUNK), preferred_element_type=F32)
        hm = (a * jax.nn.sigmoid(a) * b).astype(BF16)
        d = jnp.dot(hm, w2_ref[j], preferred_element_type=F32)
        acc = d if acc is None else acc + d
    res = h_ref[...] + 0.5 * gate * acc
    if final:
        res = _rms(res, RMS_EPS) * fg_ref[...]
    o_ref[...] = res


def _ffn_call(h2d, mod, ngain, w13, w2, fgain, *, layer, mod_base, final):
    n_tok = h2d.shape[0]
    tiles_per_seq = SEQ // FFN_ROWS
    kern = functools.partial(_ffn_kernel, layer=layer, mod_base=mod_base, final=final)
    return pl.pallas_call(
        kern,
        grid=(n_tok // FFN_ROWS,),
        in_specs=[
            pl.BlockSpec((FFN_ROWS, D_MODEL), lambda i: (i, 0)),
            pl.BlockSpec((1, N_MOD, D_MODEL), lambda i: (i // tiles_per_seq, 0, 0)),
            _resident((1, D_MODEL)),
            pl.BlockSpec(memory_space=pl.ANY),
            pl.BlockSpec(memory_space=pl.ANY),
            _resident((1, D_MODEL)),
        ],
        out_specs=pl.BlockSpec((FFN_ROWS, D_MODEL), lambda i: (i, 0)),
        out_shape=jax.ShapeDtypeStruct(h2d.shape, F32),
        scratch_shapes=[
            pltpu.VMEM((2 * D_FF // W13_BLOCK, D_MODEL, W13_BLOCK), BF16),
            pltpu.VMEM((N_FF_CHUNKS, FF_CHUNK, D_MODEL), BF16),
            pltpu.VMEM((2, D_MODEL, W13_BLOCK), F32),
            pltpu.VMEM((2, FF_CHUNK, D_MODEL), F32),
            pltpu.SemaphoreType.DMA((2,)),
            pltpu.SemaphoreType.DMA((2,)),
        ],
        compiler_params=pltpu.CompilerParams(
            dimension_semantics=("arbitrary",), vmem_limit_bytes=VMEM_LIMIT),
        name="ffn_final" if final else "ffn",
    )(h2d, mod, ngain, w13, w2, fgain)


def _hgrn_constants():
    r = np.arange(CHUNK)
    pair = np.zeros((N_LEVELS + 1, CHUNK, CHUNK), np.float32)
    prefix = (r[None, :] <= r[:, None]).astype(np.float32)
    for i, m in enumerate(HALF_SIZES):
        blk = r // (2 * m)
        low = (r % (2 * m)) < m
        pair[i] = (blk[:, None] == blk[None, :]) & (~low)[:, None] & low[None, :]
    pair[N_LEVELS] = np.eye(CHUNK)
    return np.concatenate([prefix, prefix], axis=1), pair


def _level_log_decay(b, m):
    rows, width = b.shape
    if m >= SUBLANES:
        parts = []
        for r0 in range(0, rows, 2 * m):
            bref = b[r0 + m - 1:r0 + m, :]
            parts.append(bref - b[r0:r0 + m])
            parts.append(b[r0 + m:r0 + 2 * m] - bref)
        return jnp.concatenate(parts, axis=0)
    row = lax.broadcasted_iota(jnp.int32, b.shape, 0)
    if m == 1:
        bref = pltpu.roll(b, 1, 0)
    else:
        def group_row(i):
            return jnp.concatenate(
                [jnp.broadcast_to(b[g + i:g + i + 1, :], (SUBLANES, width))
                 for g in range(0, rows, SUBLANES)], axis=0)
        bref = group_row(m - 1)
        for blk in range(1, SUBLANES // (2 * m)):
            bref = jnp.where((row % SUBLANES) >= blk * 2 * m, group_row(blk * 2 * m + m - 1), bref)
    upper = (row & m) != 0
    if m == 1:
        return jnp.where(upper, b - bref, 0.0)
    return jnp.where(upper, b - bref, bref - b)


def _select_rows(m, upper_val, lower_val):
    rows = upper_val.shape[0]
    if m >= SUBLANES:
        parts = []
        for r0 in range(0, rows, m):
            src = upper_val if (r0 // m) % 2 == 1 else lower_val
            parts.append(src[r0:r0 + m])
        return jnp.concatenate(parts, axis=0)
    row = lax.broadcasted_iota(jnp.int32, upper_val.shape, 0)
    return jnp.where((row & m) != 0, upper_val, lower_val)


def _mixer_kernel(h_ref, mod_ref, ng_ref, win_hbm, wout_hbm, lbl_ref, gn_ref, lng_ref, og_ref,
                  wsp_ref, bsp_ref, prefix_ref, pair_ref, o_ref, st_ref, proj_ref, win_ref, wout_ref,
                  stage_in_ref, stage_out_ref, sem_in_ref, sem_out_ref, *, layer):
    @pl.when((pl.program_id(0) == 0) & (pl.program_id(1) == 0))
    def _():
        _stream_cast(win_ref.shape[0], _col_block(win_hbm, layer, WIN_BLOCK), win_ref,
                     stage_in_ref, sem_in_ref)
        _stream_cast(wout_ref.shape[0], _col_block(wout_hbm, layer, WOUT_BLOCK), wout_ref,
                     stage_out_ref, sem_out_ref)

    @pl.when(pl.program_id(1) == 0)
    def _():
        st_ref[...] = jnp.zeros_like(st_ref)

    sh = mod_ref[0, 3:4, :]
    sc = mod_ref[0, 4:5, :]
    gate = mod_ref[0, 5:6, :]

    def normalize(r0):
        y = _rms(h_ref[0, r0:r0 + SUB_ROWS, :], RMS_EPS) * ng_ref[...]
        return (y * (1.0 + sc) + sh).astype(BF16)

    def proj(r0, rows, c0, c1):
        return proj_ref[r0:r0 + rows, c0:c1]

    def project(r0, y, c0, c1):
        for c in range(c0, c1, WIN_BLOCK):
            proj_ref[r0:r0 + SUB_ROWS, c:c + WIN_BLOCK] = jnp.dot(
                y, _cols(win_ref, c, WIN_BLOCK), preferred_element_type=F32)

    def out_project(r0, cat, c0, c1):
        mix = jnp.dot(cat, _cols(wout_ref, c0, c1 - c0), preferred_element_type=F32)
        o_ref[0, r0:r0 + SUB_ROWS, c0:c1] = h_ref[0, r0:r0 + SUB_ROWS, c0:c1] + gate[:, c0:c1] * mix

    lg = lbl_ref[...]
    ex = jnp.exp(lg - jnp.max(lg, axis=0, keepdims=True))
    prob = ex / jnp.sum(ex, axis=0, keepdims=True)
    lb = jnp.zeros((1, HG_WIDTH), F32)
    for i in range(1, layer + 1):
        lb = lb + prob[i:i + 1, :]

    r2 = lax.broadcasted_iota(jnp.int32, (GM_BLOCK, GM_BLOCK), 0)
    c2 = lax.broadcasted_iota(jnp.int32, (GM_BLOCK, GM_BLOCK), 1)
    causal = (r2 // CHUNK) >= (c2 // CHUNK)
    w_sp = [jnp.where(causal, wsp_ref[hd], 0.0).astype(BF16) for hd in range(GM_HEADS)]

    def gm_mix(r0, n_blocks, head_ids):
        res = [[] for _ in range(n_blocks)]
        for hd in head_ids:
            cu = 4 * HG_WIDTH + hd * GM_DH
            cv = cu + GM_WIDTH
            us, vns = [], []
            for blk in range(n_blocks):
                rb = r0 + blk * GM_BLOCK
                us.append(jax.nn.gelu(proj(rb, GM_BLOCK, cu, cu + GM_DH)))
                v = jax.nn.gelu(proj(rb, GM_BLOCK, cv, cv + GM_DH))
                vc = v - jnp.mean(v, axis=-1, keepdims=True)
                var = jnp.mean(vc * vc, axis=-1, keepdims=True)
                vn = vc * lax.rsqrt(var + LN_EPS) * lng_ref[:, hd * GM_DH:(hd + 1) * GM_DH]
                vns.append(vn.astype(BF16))
            mixed = jnp.dot(w_sp[hd], jnp.concatenate(vns, axis=1), preferred_element_type=F32)
            for blk in range(n_blocks):
                res[blk].append((us[blk], mixed[:, blk * GM_DH:(blk + 1) * GM_DH] + bsp_ref[hd]))
        return res

    def gm_gate(res):
        outs = []
        for hd, (u, mixed) in enumerate(res):
            z = _rms(u * mixed, RMS_EPS) * og_ref[:, hd * GM_DH:(hd + 1) * GM_DH]
            outs.append(z.astype(BF16))
        return jnp.concatenate(outs, axis=1)

    def hg_gates(r0):
        q = proj(r0, CHUNK, 0, HG_WIDTH)
        f_raw = proj(r0, CHUNK, HG_WIDTH, 2 * HG_WIDTH)
        vb = proj(r0, CHUNK, 2 * HG_WIDTH, 3 * HG_WIDTH).astype(BF16)
        g = proj(r0, CHUNK, 3 * HG_WIDTH, 4 * HG_WIDTH)
        g_act = g * jax.nn.sigmoid(g)
        f = lb + (1.0 - lb) * jax.nn.sigmoid(f_raw)
        log_f = jnp.log(f) * LOG2_E
        k = 1.0 - f
        hi = log_f.astype(BF16)
        mid = (log_f - hi.astype(F32)).astype(BF16)
        b = jnp.dot(prefix_ref[...], jnp.concatenate([hi, mid], axis=0),
                    preferred_element_type=F32)
        return q, k, f, vb, g_act, b

    def hg_scores(gates):
        q, k, f, vb_all, g_act, b = gates
        decay_in = jnp.exp2(b)
        decay_out = jnp.exp2(b[CHUNK - 1:CHUNK, :] - b)
        q_in = (q * decay_in).astype(BF16)
        k_out = (k * decay_out).astype(BF16)
        heads = [slice(hd * HG_DK, (hd + 1) * HG_DK) for hd in range(HG_HEADS)]
        same = q * k
        prev = q * pltpu.roll(k, 1, 0) * f
        levels = [(_select_rows(m, q, k) * jnp.exp2(_level_log_decay(b, m))).astype(BF16)
                  for m in MATMUL_SIZES]
        dn = (((1,), (1,)), ((), ()))
        per_head = []
        for cs in heads:
            att = (jnp.sum(same[:, cs], axis=-1, keepdims=True) * pair_ref[N_LEVELS]
                   + jnp.sum(prev[:, cs], axis=-1, keepdims=True) * pair_ref[N_LEVELS - 1])
            for i, level in enumerate(levels):
                mh = level[:, cs]
                att = att + lax.dot_general(mh, mh, dn, preferred_element_type=F32) * pair_ref[i]
            vb = vb_all[:, cs]
            inc = lax.dot_general(vb, k_out[:, cs], (((0,), (0,)), ((), ())),
                                  preferred_element_type=F32)
            per_head.append((att.astype(BF16), vb, q_in[:, cs], inc, decay_in[CHUNK - 1:CHUNK, cs],
                             g_act[:, cs]))
        return per_head

    def hg_output(per_head):
        dn = (((1,), (1,)), ((), ()))
        outs = []
        for hd, (att, vb, q_in, inc, decay_last, g_act) in enumerate(per_head):
            cs = slice(hd * HG_DK, (hd + 1) * HG_DK)
            st = st_ref[hd]
            o = (jnp.dot(att, vb, preferred_element_type=F32)
                 + lax.dot_general(q_in, st.astype(BF16), dn, preferred_element_type=F32))
            st_ref[hd] = st * decay_last + inc
            o = _rms(o, RMS_EPS) * gn_ref[:, cs] * g_act
            outs.append(o.astype(BF16))
        return jnp.concatenate(outs, axis=1)

    n_sub = MIX_ROWS // SUB_ROWS
    n_chunks = MIX_ROWS // CHUNK
    chunks_per_sub = SUB_ROWS // CHUNK
    chunks_per_blk = GM_BLOCK // CHUNK
    blocks_per_sub = SUB_ROWS // GM_BLOCK
    in_piece = IN_COLS // chunks_per_sub
    out_piece = D_MODEL // chunks_per_sub
    lag_out = 2

    y = [normalize(s * SUB_ROWS) for s in range(n_sub)]
    project(0, y[0], 0, IN_COLS)
    gates, scores, gm_state, hg_out, gm_out, cat = {}, {}, {}, {}, {}, {}
    def big_projections(slot):
        s_next = slot // chunks_per_sub + 1
        if s_next < n_sub:
            p = slot % chunks_per_sub
            project(s_next * SUB_ROWS, y[s_next], p * in_piece, (p + 1) * in_piece)
        done = slot - lag_out - chunks_per_sub
        if done >= 0 and done // chunks_per_sub < n_sub:
            s_done, p = divmod(done, chunks_per_sub)
            if p == 0:
                c0 = s_done * chunks_per_sub
                hg_part = jnp.concatenate([hg_out.pop(c0 + i) for i in range(chunks_per_sub)], axis=0)
                gm_part = jnp.concatenate(
                    [gm_out.pop(c0 + i) for i in range(0, chunks_per_sub, chunks_per_blk)], axis=0)
                cat[s_done] = jnp.concatenate([hg_part, gm_part], axis=1)
            out_project(s_done * SUB_ROWS, cat[s_done], p * out_piece, (p + 1) * out_piece)

    for slot in range(n_chunks + lag_out + chunks_per_sub):
        big_projections(slot)
        if 0 <= slot - lag_out < n_chunks:
            hg_out[slot - lag_out] = hg_output(scores.pop(slot - lag_out))
        if 0 <= slot - 1 < n_chunks:
            scores[slot - 1] = hg_scores(gates.pop(slot - 1))
        if slot < n_chunks:
            gates[slot] = hg_gates(slot * CHUNK)
            p = slot % chunks_per_sub
            if p < 2:
                head_ids = range(p * GM_HEADS // 2, (p + 1) * GM_HEADS // 2)
                for i, res in enumerate(gm_mix((slot - p) * CHUNK, blocks_per_sub, head_ids)):
                    gm_state.setdefault(slot - p + i * chunks_per_blk, []).extend(res)
            else:
                key = slot - p + (p - 2) * chunks_per_blk
                gm_out[key] = gm_gate(gm_state.pop(key))


def _mixer_call(h, mod, ngain, w_in, w_out, lb_logits, gnorm, ln_gain, out_gain, w_sp, b_sp,
                prefix_c, pair_c, *, layer):
    batch = h.shape[0]
    depth = lb_logits.shape[0]
    kern = functools.partial(_mixer_kernel, layer=layer)
    return pl.pallas_call(
        kern,
        grid=(batch, SEQ // MIX_ROWS),
        in_specs=[
            pl.BlockSpec((1, MIX_ROWS, D_MODEL), lambda b, j: (b, j, 0)),
            pl.BlockSpec((1, N_MOD, D_MODEL), lambda b, j: (b, 0, 0)),
            _resident((1, D_MODEL)),
            pl.BlockSpec(memory_space=pl.ANY),
            pl.BlockSpec(memory_space=pl.ANY),
            _resident((depth, HG_WIDTH)),
            _resident((1, HG_WIDTH)),
            _resident((1, GM_WIDTH)),
            _resident((1, GM_WIDTH)),
            _resident((GM_HEADS, GM_BLOCK, GM_BLOCK)),
            _resident((GM_HEADS, GM_BLOCK, 1)),
            _resident(prefix_c.shape),
            _resident(pair_c.shape),
        ],
        out_specs=pl.BlockSpec((1, MIX_ROWS, D_MODEL), lambda b, j: (b, j, 0)),
        out_shape=jax.ShapeDtypeStruct(h.shape, F32),
        scratch_shapes=[pltpu.VMEM((HG_HEADS, HG_DK, HG_DK), F32),
                        pltpu.VMEM((MIX_ROWS, IN_COLS), F32),
                        pltpu.VMEM((IN_COLS // WIN_BLOCK, D_MODEL, WIN_BLOCK), BF16),
                        pltpu.VMEM((D_MODEL // WOUT_BLOCK, D_MODEL, WOUT_BLOCK), BF16),
                        pltpu.VMEM((2, D_MODEL, WIN_BLOCK), F32),
                        pltpu.VMEM((2, D_MODEL, WOUT_BLOCK), F32),
                        pltpu.SemaphoreType.DMA((2,)),
                        pltpu.SemaphoreType.DMA((2,))],
        compiler_params=pltpu.CompilerParams(
            dimension_semantics=("arbitrary", "arbitrary"), vmem_limit_bytes=VMEM_LIMIT),
        name="mixer",
    )(h, mod, ngain, w_in, w_out, lb_logits, gnorm, ln_gain, out_gain, w_sp, b_sp, prefix_c, pair_c)


def kernel(x, c, w_ada, b_ada, norm_gain, ffn1_w13, ffn1_w2, w_in, hg_lb_logits, hg_gnorm,
           gm_ln_gain, gm_w_spatial, gm_b_spatial, gm_out_gain, w_out, ffn2_w13, ffn2_w2,
           final_gain):
    batch, seq, _ = x.shape
    depth = w_ada.shape[0]
    prefix_np, pair_np = _hgrn_constants()
    prefix_c = jnp.asarray(prefix_np, BF16)
    pair_c = jnp.asarray(pair_np, F32)
    fgain = final_gain.reshape(1, D_MODEL)

    mod_all = _ada_call(c, w_ada, b_ada).reshape(depth, batch, N_MOD, D_MODEL)
    h = x
    for l in range(depth):
        mod = mod_all[l]
        h = _ffn_call(h.reshape(batch * seq, D_MODEL), mod, norm_gain[l, 0:1], ffn1_w13, ffn1_w2,
                      fgain, layer=l, mod_base=0, final=False).reshape(batch, seq, D_MODEL)
        h = _mixer_call(h, mod, norm_gain[l, 1:2], w_in, w_out,
                        hg_lb_logits, hg_gnorm[l:l + 1], gm_ln_gain[l:l + 1], gm_out_gain[l:l + 1],
                        gm_w_spatial[l], gm_b_spatial[l].reshape(GM_HEADS, GM_BLOCK, 1),
                        prefix_c, pair_c, layer=l)
        h = _ffn_call(h.reshape(batch * seq, D_MODEL), mod, norm_gain[l, 2:3], ffn2_w13, ffn2_w2,
                      fgain, layer=l, mod_base=6,
                      final=(l == depth - 1)).reshape(batch, seq, D_MODEL)
    return h
```

```python
import functools

import numpy as np
import jax
import jax.numpy as jnp
from jax import lax
from jax.experimental import pallas as pl
from jax.experimental.pallas import tpu as pltpu

D_MODEL = 1024
SEQ = 2048
CHUNK = 64
HG_WIDTH = 512
HG_HEADS = 4
HG_DK = 128
GM_WIDTH = 512
GM_HEADS = 4
GM_DH = 128
GM_BLOCK = 128
D_FF = 2816
N_MOD = 9
IN_COLS = 4 * HG_WIDTH + 2 * GM_WIDTH
RMS_EPS = 1e-6
LN_EPS = 1e-5

FF_CHUNK = 256
N_FF_CHUNKS = D_FF // FF_CHUNK
W13_BLOCK = 2 * FF_CHUNK
WIN_BLOCK = 768
WOUT_BLOCK = 256
FFN_ROWS = 512
MIX_ROWS = 512
SUB_ROWS = 256
ADA_COLS = 2304
NEAR_ROWS = 4
MATMUL_SIZES = (32, 16, 8, 4)
SUBLANES = 8
VMEM_LIMIT = 56 * 1024 * 1024
LOG2_E = 1.4426950408889634

F32 = jnp.float32
BF16 = jnp.bfloat16


def _rms(x, eps):
    return x * lax.rsqrt(jnp.mean(x * x, axis=-1, keepdims=True) + eps)


def _resident(shape):
    return pl.BlockSpec(shape, lambda *_: (0,) * len(shape), pipeline_mode=pl.Buffered(1))


def _ada_kernel(c_ref, w_ref, b_ref, o_ref):
    c = c_ref[...]
    ca = c * jax.nn.sigmoid(c)
    batch = ca.shape[0]
    ca_hi = ca.astype(BF16)
    ca_lo = (ca - ca_hi.astype(F32)).astype(BF16)
    w = w_ref[0]
    w_hi = w.astype(BF16)
    w_lo = (w - w_hi.astype(F32)).astype(BF16)
    p = jnp.dot(jnp.concatenate([ca_hi, ca_lo], axis=0), w_hi, preferred_element_type=F32)
    q = jnp.dot(ca_hi, w_lo, preferred_element_type=F32)
    o_ref[0] = p[:batch] + p[batch:] + q + b_ref[0]


def _ada_call(c, w_ada, b_ada):
    depth, _, n_out = w_ada.shape
    batch = c.shape[0]
    return pl.pallas_call(
        _ada_kernel,
        grid=(depth, n_out // ADA_COLS),
        in_specs=[
            pl.BlockSpec((batch, D_MODEL), lambda l, j: (0, 0)),
            pl.BlockSpec((1, D_MODEL, ADA_COLS), lambda l, j: (l, 0, j)),
            pl.BlockSpec((1, 1, ADA_COLS), lambda l, j: (l, 0, j)),
        ],
        out_specs=pl.BlockSpec((1, batch, ADA_COLS), lambda l, j: (l, 0, j)),
        out_shape=jax.ShapeDtypeStruct((depth, batch, n_out), F32),
        compiler_params=pltpu.CompilerParams(
            dimension_semantics=("arbitrary", "arbitrary"), vmem_limit_bytes=VMEM_LIMIT),
        name="ada",
    )(c, w_ada, b_ada.reshape(depth, 1, n_out))


def _stream_cast(n_blocks, src_block, dst_ref, stage_ref, sem_ref):
    def copy(i, slot):
        return pltpu.make_async_copy(src_block(i), stage_ref.at[slot], sem_ref.at[slot])

    copy(0, 0).start()

    def body(i, carry):
        slot = i % 2

        @pl.when(i + 1 < n_blocks)
        def _():
            copy(i + 1, 1 - slot).start()

        copy(i, slot).wait()
        dst_ref[i] = stage_ref[slot].astype(BF16)
        return carry

    lax.fori_loop(0, n_blocks, body, 0)


def _col_block(w_hbm, layer, width):
    return lambda i: w_hbm.at[layer, :, pl.ds(pl.multiple_of(i * width, width), width)]


def _row_block(w_hbm, layer, height):
    return lambda i: w_hbm.at[layer, pl.ds(pl.multiple_of(i * height, height), height), :]


def _cols(w_ref, c0, width):
    block_width = w_ref.shape[2]
    blk, off = divmod(c0, block_width)
    assert off + width <= block_width
    return w_ref[blk, :, off:off + width]


def _ffn_kernel(h_ref, mod_ref, ng_ref, w13_hbm, w2_hbm, fg_ref, o_ref, w13_ref, w2_ref,
                stage13_ref, stage2_ref, sem13_ref, sem2_ref, *, layer, mod_base, final):
    @pl.when(pl.program_id(0) == 0)
    def _():
        _stream_cast(w13_ref.shape[0], _col_block(w13_hbm, layer, W13_BLOCK), w13_ref,
                     stage13_ref, sem13_ref)
        _stream_cast(w2_ref.shape[0], _row_block(w2_hbm, layer, FF_CHUNK), w2_ref,
                     stage2_ref, sem2_ref)

    h = h_ref[...]
    sh = mod_ref[0, mod_base:mod_base + 1, :]
    sc = mod_ref[0, mod_base + 1:mod_base + 2, :]
    gate = mod_ref[0, mod_base + 2:mod_base + 3, :]
    y = _rms(h, RMS_EPS) * ng_ref[...]
    yb = (y * (1.0 + sc) + sh).astype(BF16)
    acc = None
    for j in range(N_FF_CHUNKS):
        c0 = j * FF_CHUNK
        a = jnp.dot(yb, _cols(w13_ref, c0, FF_CHUNK), preferred_element_type=F32)
        b = jnp.dot(yb, _cols(w13_ref, D_FF + c0, FF_CHUNK), preferred_element_type=F32)
        hm = (a * jax.nn.sigmoid(a) * b).astype(BF16)
        d = jnp.dot(hm, w2_ref[j], preferred_element_type=F32)
        acc = d if acc is None else acc + d
    res = h_ref[...] + 0.5 * gate * acc
    if final:
        res = _rms(res, RMS_EPS) * fg_ref[...]
    o_ref[...] = res


def _ffn_call(h2d, mod, ngain, w13, w2, fgain, *, layer, mod_base, final):
    n_tok = h2d.shape[0]
    tiles_per_seq = SEQ // FFN_ROWS
    kern = functools.partial(_ffn_kernel, layer=layer, mod_base=mod_base, final=final)
    return pl.pallas_call(
        kern,
        grid=(n_tok // FFN_ROWS,),
        in_specs=[
            pl.BlockSpec((FFN_ROWS, D_MODEL), lambda i: (i, 0)),
            pl.BlockSpec((1, N_MOD, D_MODEL), lambda i: (i // tiles_per_seq, 0, 0)),
            _resident((1, D_MODEL)),
            pl.BlockSpec(memory_space=pl.ANY),
            pl.BlockSpec(memory_space=pl.ANY),
            _resident((1, D_MODEL)),
        ],
        out_specs=pl.BlockSpec((FFN_ROWS, D_MODEL), lambda i: (i, 0)),
        out_shape=jax.ShapeDtypeStruct(h2d.shape, F32),
        scratch_shapes=[
            pltpu.VMEM((2 * D_FF // W13_BLOCK, D_MODEL, W13_BLOCK), BF16),
            pltpu.VMEM((N_FF_CHUNKS, FF_CHUNK, D_MODEL), BF16),
            pltpu.VMEM((2, D_MODEL, W13_BLOCK), F32),
            pltpu.VMEM((2, FF_CHUNK, D_MODEL), F32),
            pltpu.SemaphoreType.DMA((2,)),
            pltpu.SemaphoreType.DMA((2,)),
        ],
        compiler_params=pltpu.CompilerParams(
            dimension_semantics=("arbitrary",), vmem_limit_bytes=VMEM_LIMIT),
        name="ffn_final" if final else "ffn",
    )(h2d, mod, ngain, w13, w2, fgain)


def _hgrn_constants():
    r = np.arange(CHUNK)
    prefix = (r[None, :] <= r[:, None]).astype(np.float32)
    pair = np.zeros((len(MATMUL_SIZES), CHUNK, CHUNK), np.float32)
    for i, m in enumerate(MATMUL_SIZES):
        blk = r // (2 * m)
        low = (r % (2 * m)) < m
        pair[i] = (blk[:, None] == blk[None, :]) & (~low)[:, None] & low[None, :]
    near = np.zeros((NEAR_ROWS, CHUNK, CHUNK), np.float32)
    for d in range(NEAR_ROWS):
        near[d] = (r[:, None] - r[None, :] == d) & ((r % NEAR_ROWS) >= d)[:, None]
    return np.concatenate([prefix, prefix], axis=1), np.concatenate([pair, near], axis=0)


def _level_log_decay(b, m):
    rows, width = b.shape
    if m >= SUBLANES:
        parts = []
        for r0 in range(0, rows, 2 * m):
            bref = b[r0 + m - 1:r0 + m, :]
            parts.append(bref - b[r0:r0 + m])
            parts.append(b[r0 + m:r0 + 2 * m] - bref)
        return jnp.concatenate(parts, axis=0)
    row = lax.broadcasted_iota(jnp.int32, b.shape, 0)

    def group_row(i):
        return jnp.concatenate(
            [jnp.broadcast_to(b[g + i:g + i + 1, :], (SUBLANES, width))
             for g in range(0, rows, SUBLANES)], axis=0)

    bref = group_row(m - 1)
    for blk in range(1, SUBLANES // (2 * m)):
        bref = jnp.where((row % SUBLANES) >= blk * 2 * m, group_row(blk * 2 * m + m - 1), bref)
    return jnp.where((row & m) != 0, b - bref, bref - b)


def _select_rows(m, upper_val, lower_val):
    rows = upper_val.shape[0]
    if m >= SUBLANES:
        parts = []
        for r0 in range(0, rows, m):
            src = upper_val if (r0 // m) % 2 == 1 else lower_val
            parts.append(src[r0:r0 + m])
        return jnp.concatenate(parts, axis=0)
    row = lax.broadcasted_iota(jnp.int32, upper_val.shape, 0)
    return jnp.where((row & m) != 0, upper_val, lower_val)


def _mixer_kernel(h_ref, mod_ref, ng_ref, win_hbm, wout_hbm, lbl_ref, gn_ref, lng_ref, og_ref,
                  wsp_ref, bsp_ref, prefix_ref, pair_ref, o_ref, st_ref, proj_ref, win_ref, wout_ref,
                  stage_in_ref, stage_out_ref, sem_in_ref, sem_out_ref, *, layer):
    @pl.when((pl.program_id(0) == 0) & (pl.program_id(1) == 0))
    def _():
        _stream_cast(win_ref.shape[0], _col_block(win_hbm, layer, WIN_BLOCK), win_ref,
                     stage_in_ref, sem_in_ref)
        _stream_cast(wout_ref.shape[0], _col_block(wout_hbm, layer, WOUT_BLOCK), wout_ref,
                     stage_out_ref, sem_out_ref)

    @pl.when(pl.program_id(1) == 0)
    def _():
        st_ref[...] = jnp.zeros_like(st_ref)

    sh = mod_ref[0, 3:4, :]
    sc = mod_ref[0, 4:5, :]
    gate = mod_ref[0, 5:6, :]

    def normalize(r0):
        y = _rms(h_ref[0, r0:r0 + SUB_ROWS, :], RMS_EPS) * ng_ref[...]
        return (y * (1.0 + sc) + sh).astype(BF16)

    def proj(r0, rows, c0, c1):
        return proj_ref[r0:r0 + rows, c0:c1]

    def project(r0, y, c0, c1):
        for c in range(c0, c1, WIN_BLOCK):
            proj_ref[r0:r0 + SUB_ROWS, c:c + WIN_BLOCK] = jnp.dot(
                y, _cols(win_ref, c, WIN_BLOCK), preferred_element_type=F32)

    def out_project(r0, cat, c0, c1):
        mix = jnp.dot(cat, _cols(wout_ref, c0, c1 - c0), preferred_element_type=F32)
        o_ref[0, r0:r0 + SUB_ROWS, c0:c1] = h_ref[0, r0:r0 + SUB_ROWS, c0:c1] + gate[:, c0:c1] * mix

    lg = lbl_ref[...]
    ex = jnp.exp(lg - jnp.max(lg, axis=0, keepdims=True))
    prob = ex / jnp.sum(ex, axis=0, keepdims=True)
    lb = jnp.zeros((1, HG_WIDTH), F32)
    for i in range(1, layer + 1):
        lb = lb + prob[i:i + 1, :]

    r2 = lax.broadcasted_iota(jnp.int32, (GM_BLOCK, GM_BLOCK), 0)
    c2 = lax.broadcasted_iota(jnp.int32, (GM_BLOCK, GM_BLOCK), 1)
    causal = (r2 // CHUNK) >= (c2 // CHUNK)
    w_sp = [jnp.where(causal, wsp_ref[hd], 0.0).astype(BF16) for hd in range(GM_HEADS)]

    def gm_mix(r0, n_blocks, head_ids):
        res = [[] for _ in range(n_blocks)]
        for hd in head_ids:
            cu = 4 * HG_WIDTH + hd * GM_DH
            cv = cu + GM_WIDTH
            us, vns = [], []
            for blk in range(n_blocks):
                rb = r0 + blk * GM_BLOCK
                us.append(jax.nn.gelu(proj(rb, GM_BLOCK, cu, cu + GM_DH)))
                v = jax.nn.gelu(proj(rb, GM_BLOCK, cv, cv + GM_DH))
                vc = v - jnp.mean(v, axis=-1, keepdims=True)
                var = jnp.mean(vc * vc, axis=-1, keepdims=True)
                vn = vc * lax.rsqrt(var + LN_EPS) * lng_ref[:, hd * GM_DH:(hd + 1) * GM_DH]
                vns.append(vn.astype(BF16))
            mixed = jnp.dot(w_sp[hd], jnp.concatenate(vns, axis=1), preferred_element_type=F32)
            for blk in range(n_blocks):
                res[blk].append((us[blk], mixed[:, blk * GM_DH:(blk + 1) * GM_DH] + bsp_ref[hd]))
        return res

    def gm_gate(res):
        outs = []
        for hd, (u, mixed) in enumerate(res):
            z = _rms(u * mixed, RMS_EPS) * og_ref[:, hd * GM_DH:(hd + 1) * GM_DH]
            outs.append(z.astype(BF16))
        return jnp.concatenate(outs, axis=1)

    def hg_gates(r0):
        q = proj(r0, CHUNK, 0, HG_WIDTH)
        f_raw = proj(r0, CHUNK, HG_WIDTH, 2 * HG_WIDTH)
        vb = proj(r0, CHUNK, 2 * HG_WIDTH, 3 * HG_WIDTH).astype(BF16)
        g = proj(r0, CHUNK, 3 * HG_WIDTH, 4 * HG_WIDTH)
        g_act = g * jax.nn.sigmoid(g)
        f = lb + (1.0 - lb) * jax.nn.sigmoid(f_raw)
        log_f = jnp.log(f) * LOG2_E
        k = 1.0 - f
        hi = log_f.astype(BF16)
        mid = (log_f - hi.astype(F32)).astype(BF16)
        b = jnp.dot(prefix_ref[...], jnp.concatenate([hi, mid], axis=0),
                    preferred_element_type=F32)
        return q, k, f, vb, g_act, b

    def hg_scores(gates):
        q, k, f, vb_all, g_act, b = gates
        decay_in = jnp.exp2(b)
        decay_out = jnp.exp2(b[CHUNK - 1:CHUNK, :] - b)
        q_in = (q * decay_in).astype(BF16)
        k_out = (k * decay_out).astype(BF16)
        heads = [slice(hd * HG_DK, (hd + 1) * HG_DK) for hd in range(HG_HEADS)]
        def earlier(x, d):
            grouped = x.reshape(x.shape[0] // SUBLANES, SUBLANES, x.shape[1])
            return pltpu.roll(grouped, d, 1).reshape(x.shape)

        near = [q * k]
        decay = None
        for d in range(1, NEAR_ROWS):
            decay = f if d == 1 else decay * earlier(f, d - 1)
            near.append(q * earlier(k, d) * decay)
        levels = [(_select_rows(m, q, k) * jnp.exp2(_level_log_decay(b, m))).astype(BF16)
                  for m in MATMUL_SIZES]
        dn = (((1,), (1,)), ((), ()))
        per_head = []
        for cs in heads:
            att = sum(jnp.sum(prod[:, cs], axis=-1, keepdims=True) * pair_ref[len(MATMUL_SIZES) + d]
                      for d, prod in enumerate(near))
            for i, level in enumerate(levels):
                mh = level[:, cs]
                att = att + lax.dot_general(mh, mh, dn, preferred_element_type=F32) * pair_ref[i]
            vb = vb_all[:, cs]
            inc = lax.dot_general(vb, k_out[:, cs], (((0,), (0,)), ((), ())),
                                  preferred_element_type=F32)
            per_head.append((att.astype(BF16), vb, q_in[:, cs], inc, decay_in[CHUNK - 1:CHUNK, cs],
                             g_act[:, cs]))
        return per_head

    def hg_output(per_head):
        dn = (((1,), (1,)), ((), ()))
        outs = []
        for hd, (att, vb, q_in, inc, decay_last, g_act) in enumerate(per_head):
            cs = slice(hd * HG_DK, (hd + 1) * HG_DK)
            st = st_ref[hd]
            o = (jnp.dot(att, vb, preferred_element_type=F32)
                 + lax.dot_general(q_in, st.astype(BF16), dn, preferred_element_type=F32))
            st_ref[hd] = st * decay_last + inc
            o = _rms(o, RMS_EPS) * gn_ref[:, cs] * g_act
            outs.append(o.astype(BF16))
        return jnp.concatenate(outs, axis=1)

    n_sub = MIX_ROWS // SUB_ROWS
    n_chunks = MIX_ROWS // CHUNK
    chunks_per_sub = SUB_ROWS // CHUNK
    chunks_per_blk = GM_BLOCK // CHUNK
    blocks_per_sub = SUB_ROWS // GM_BLOCK
    in_piece = IN_COLS // chunks_per_sub
    out_piece = D_MODEL // chunks_per_sub
    lag_out = 2

    y = [normalize(s * SUB_ROWS) for s in range(n_sub)]
    project(0, y[0], 0, IN_COLS)
    gates, scores, gm_state, hg_out, gm_out, cat = {}, {}, {}, {}, {}, {}
    def big_projections(slot):
        s_next = slot // chunks_per_sub + 1
        if s_next < n_sub:
            p = slot % chunks_per_sub
            project(s_next * SUB_ROWS, y[s_next], p * in_piece, (p + 1) * in_piece)
        done = slot - lag_out - chunks_per_sub
        if done >= 0 and done // chunks_per_sub < n_sub:
            s_done, p = divmod(done, chunks_per_sub)
            if p == 0:
                c0 = s_done * chunks_per_sub
                hg_part = jnp.concatenate([hg_out.pop(c0 + i) for i in range(chunks_per_sub)], axis=0)
                gm_part = jnp.concatenate(
                    [gm_out.pop(c0 + i) for i in range(0, chunks_per_sub, chunks_per_blk)], axis=0)
                cat[s_done] = jnp.concatenate([hg_part, gm_part], axis=1)
            out_project(s_done * SUB_ROWS, cat[s_done], p * out_piece, (p + 1) * out_piece)

    for slot in range(n_chunks + lag_out + chunks_per_sub):
        big_projections(slot)
        if 0 <= slot - lag_out < n_chunks:
            hg_out[slot - lag_out] = hg_output(scores.pop(slot - lag_out))
        if 0 <= slot - 1 < n_chunks:
            scores[slot - 1] = hg_scores(gates.pop(slot - 1))
        if slot < n_chunks:
            gates[slot] = hg_gates(slot * CHUNK)
            p = slot % chunks_per_sub
            if p < 2:
                head_ids = range(p * GM_HEADS // 2, (p + 1) * GM_HEADS // 2)
                for i, res in enumerate(gm_mix((slot - p) * CHUNK, blocks_per_sub, head_ids)):
                    gm_state.setdefault(slot - p + i * chunks_per_blk, []).extend(res)
            else:
                key = slot - p + (p - 2) * chunks_per_blk
                gm_out[key] = gm_gate(gm_state.pop(key))


def _mixer_call(h, mod, ngain, w_in, w_out, lb_logits, gnorm, ln_gain, out_gain, w_sp, b_sp,
                prefix_c, pair_c, *, layer):
    batch = h.shape[0]
    depth = lb_logits.shape[0]
    kern = functools.partial(_mixer_kernel, layer=layer)
    return pl.pallas_call(
        kern,
        grid=(batch, SEQ // MIX_ROWS),
        in_specs=[
            pl.BlockSpec((1, MIX_ROWS, D_MODEL), lambda b, j: (b, j, 0)),
            pl.BlockSpec((1, N_MOD, D_MODEL), lambda b, j: (b, 0, 0)),
            _resident((1, D_MODEL)),
            pl.BlockSpec(memory_space=pl.ANY),
            pl.BlockSpec(memory_space=pl.ANY),
            _resident((depth, HG_WIDTH)),
            _resident((1, HG_WIDTH)),
            _resident((1, GM_WIDTH)),
            _resident((1, GM_WIDTH)),
            _resident((GM_HEADS, GM_BLOCK, GM_BLOCK)),
            _resident((GM_HEADS, GM_BLOCK, 1)),
            _resident(prefix_c.shape),
            _resident(pair_c.shape),
        ],
        out_specs=pl.BlockSpec((1, MIX_ROWS, D_MODEL), lambda b, j: (b, j, 0)),
        out_shape=jax.ShapeDtypeStruct(h.shape, F32),
        scratch_shapes=[pltpu.VMEM((HG_HEADS, HG_DK, HG_DK), F32),
                        pltpu.VMEM((MIX_ROWS, IN_COLS), F32),
                        pltpu.VMEM((IN_COLS // WIN_BLOCK, D_MODEL, WIN_BLOCK), BF16),
                        pltpu.VMEM((D_MODEL // WOUT_BLOCK, D_MODEL, WOUT_BLOCK), BF16),
                        pltpu.VMEM((2, D_MODEL, WIN_BLOCK), F32),
                        pltpu.VMEM((2, D_MODEL, WOUT_BLOCK), F32),
                        pltpu.SemaphoreType.DMA((2,)),
                        pltpu.SemaphoreType.DMA((2,))],
        compiler_params=pltpu.CompilerParams(
            dimension_semantics=("arbitrary", "arbitrary"), vmem_limit_bytes=VMEM_LIMIT),
        name="mixer",
    )(h, mod, ngain, w_in, w_out, lb_logits, gnorm, ln_gain, out_gain, w_sp, b_sp, prefix_c, pair_c)


def kernel(x, c, w_ada, b_ada, norm_gain, ffn1_w13, ffn1_w2, w_in, hg_lb_logits, hg_gnorm,
           gm_ln_gain, gm_w_spatial, gm_b_spatial, gm_out_gain, w_out, ffn2_w13, ffn2_w2,
           final_gain):
    batch, seq, _ = x.shape
    depth = w_ada.shape[0]
    prefix_np, pair_np = _hgrn_constants()
    prefix_c = jnp.asarray(prefix_np, BF16)
    pair_c = jnp.asarray(pair_np, F32)
    fgain = final_gain.reshape(1, D_MODEL)

    mod_all = _ada_call(c, w_ada, b_ada).reshape(depth, batch, N_MOD, D_MODEL)
    h = x
    for l in range(depth):
        mod = mod_all[l]
        h = _ffn_call(h.reshape(batch * seq, D_MODEL), mod, norm_gain[l, 0:1], ffn1_w13, ffn1_w2,
                      fgain, layer=l, mod_base=0, final=False).reshape(batch, seq, D_MODEL)
        h = _mixer_call(h, mod, norm_gain[l, 1:2], w_in, w_out,
                        hg_lb_logits, hg_gnorm[l:l + 1], gm_ln_gain[l:l + 1], gm_out_gain[l:l + 1],
                        gm_w_spatial[l], gm_b_spatial[l].reshape(GM_HEADS, GM_BLOCK, 1),
                        prefix_c, pair_c, layer=l)
        h = _ffn_call(h.reshape(batch * seq, D_MODEL), mod, norm_gain[l, 2:3], ffn2_w13, ffn2_w2,
                      fgain, layer=l, mod_base=6,
                      final=(l == depth - 1)).reshape(batch, seq, D_MODEL)
    return h
```

```python
import functools

import numpy as np
import jax
import jax.numpy as jnp
from jax import lax
from jax.experimental import pallas as pl
from jax.experimental.pallas import tpu as pltpu

D_MODEL = 1024
SEQ = 2048
CHUNK = 64
HG_WIDTH = 512
HG_HEADS = 4
HG_DK = 128
GM_WIDTH = 512
GM_HEADS = 4
GM_DH = 128
GM_BLOCK = 128
D_FF = 2816
N_MOD = 9
IN_COLS = 4 * HG_WIDTH + 2 * GM_WIDTH
RMS_EPS = 1e-6
LN_EPS = 1e-5

FF_CHUNK = 256
N_FF_CHUNKS = D_FF // FF_CHUNK
W13_BLOCK = 2 * FF_CHUNK
WIN_BLOCK = 768
WOUT_BLOCK = 256
FFN_ROWS = 1024
MIX_ROWS = 512
SUB_ROWS = 256
ADA_COLS = 2304
NEAR_ROWS = 4
MATMUL_SIZES = (32, 16, 8, 4)
SUBLANES = 8
VMEM_LIMIT = 56 * 1024 * 1024
LOG2_E = 1.4426950408889634

F32 = jnp.float32
BF16 = jnp.bfloat16


def _rms(x, eps):
    return x * lax.rsqrt(jnp.mean(x * x, axis=-1, keepdims=True) + eps)


def _resident(shape):
    return pl.BlockSpec(shape, lambda *_: (0,) * len(shape), pipeline_mode=pl.Buffered(1))


def _ada_kernel(c_ref, w_ref, b_ref, o_ref):
    c = c_ref[...]
    ca = c * jax.nn.sigmoid(c)
    batch = ca.shape[0]
    ca_hi = ca.astype(BF16)
    ca_lo = (ca - ca_hi.astype(F32)).astype(BF16)
    w = w_ref[0]
    w_hi = w.astype(BF16)
    w_lo = (w - w_hi.astype(F32)).astype(BF16)
    p = jnp.dot(jnp.concatenate([ca_hi, ca_lo], axis=0), w_hi, preferred_element_type=F32)
    q = jnp.dot(ca_hi, w_lo, preferred_element_type=F32)
    o_ref[0] = p[:batch] + p[batch:] + q + b_ref[0]


def _ada_call(c, w_ada, b_ada):
    depth, _, n_out = w_ada.shape
    batch = c.shape[0]
    return pl.pallas_call(
        _ada_kernel,
        grid=(depth, n_out // ADA_COLS),
        in_specs=[
            pl.BlockSpec((batch, D_MODEL), lambda l, j: (0, 0)),
            pl.BlockSpec((1, D_MODEL, ADA_COLS), lambda l, j: (l, 0, j)),
            pl.BlockSpec((1, 1, ADA_COLS), lambda l, j: (l, 0, j)),
        ],
        out_specs=pl.BlockSpec((1, batch, ADA_COLS), lambda l, j: (l, 0, j)),
        out_shape=jax.ShapeDtypeStruct((depth, batch, n_out), F32),
        compiler_params=pltpu.CompilerParams(
            dimension_semantics=("arbitrary", "arbitrary"), vmem_limit_bytes=VMEM_LIMIT),
        name="ada",
    )(c, w_ada, b_ada.reshape(depth, 1, n_out))


def _stream_cast(n_blocks, src_block, dst_ref, stage_ref, sem_ref):
    def copy(i, slot):
        return pltpu.make_async_copy(src_block(i), stage_ref.at[slot], sem_ref.at[slot])

    copy(0, 0).start()

    def body(i, carry):
        slot = i % 2

        @pl.when(i + 1 < n_blocks)
        def _():
            copy(i + 1, 1 - slot).start()

        copy(i, slot).wait()
        dst_ref[i] = stage_ref[slot].astype(BF16)
        return carry

    lax.fori_loop(0, n_blocks, body, 0)


def _col_block(w_hbm, layer, width):
    return lambda i: w_hbm.at[layer, :, pl.ds(pl.multiple_of(i * width, width), width)]


def _row_block(w_hbm, layer, height):
    return lambda i: w_hbm.at[layer, pl.ds(pl.multiple_of(i * height, height), height), :]


def _cols(w_ref, c0, width):
    block_width = w_ref.shape[2]
    blk, off = divmod(c0, block_width)
    assert off + width <= block_width
    return w_ref[blk, :, off:off + width]


def _ffn_kernel(h_ref, mod_ref, ng_ref, w13_hbm, w2_hbm, fg_ref, o_ref, w13_ref, w2_ref,
                stage13_ref, stage2_ref, sem13_ref, sem2_ref, *, layer, mod_base, final):
    @pl.when(pl.program_id(0) == 0)
    def _():
        _stream_cast(w13_ref.shape[0], _col_block(w13_hbm, layer, W13_BLOCK), w13_ref,
                     stage13_ref, sem13_ref)
        _stream_cast(w2_ref.shape[0], _row_block(w2_hbm, layer, FF_CHUNK), w2_ref,
                     stage2_ref, sem2_ref)

    h = h_ref[...]
    sh = mod_ref[0, mod_base:mod_base + 1, :]
    sc = mod_ref[0, mod_base + 1:mod_base + 2, :]
    gate = mod_ref[0, mod_base + 2:mod_base + 3, :]
    y = _rms(h, RMS_EPS) * ng_ref[...]
    yb = (y * (1.0 + sc) + sh).astype(BF16)
    acc = None
    for j in range(N_FF_CHUNKS):
        c0 = j * FF_CHUNK
        a = jnp.dot(yb, _cols(w13_ref, c0, FF_CHUNK), preferred_element_type=F32)
        b = jnp.dot(yb, _cols(w13_ref, D_FF + c0, FF_CHUNK), preferred_element_type=F32)
        hm = (a * jax.nn.sigmoid(a) * b).astype(BF16)
        d = jnp.dot(hm, w2_ref[j], preferred_element_type=F32)
        acc = d if acc is None else acc + d
    res = h_ref[...] + 0.5 * gate * acc
    if final:
        res = _rms(res, RMS_EPS) * fg_ref[...]
    o_ref[...] = res


def _ffn_call(h2d, mod, ngain, w13, w2, fgain, *, layer, mod_base, final):
    n_tok = h2d.shape[0]
    tiles_per_seq = SEQ // FFN_ROWS
    kern = functools.partial(_ffn_kernel, layer=layer, mod_base=mod_base, final=final)
    return pl.pallas_call(
        kern,
        grid=(n_tok // FFN_ROWS,),
        in_specs=[
            pl.BlockSpec((FFN_ROWS, D_MODEL), lambda i: (i, 0)),
            pl.BlockSpec((1, N_MOD, D_MODEL), lambda i: (i // tiles_per_seq, 0, 0)),
            _resident((1, D_MODEL)),
            pl.BlockSpec(memory_space=pl.ANY),
            pl.BlockSpec(memory_space=pl.ANY),
            _resident((1, D_MODEL)),
        ],
        out_specs=pl.BlockSpec((FFN_ROWS, D_MODEL), lambda i: (i, 0)),
        out_shape=jax.ShapeDtypeStruct(h2d.shape, F32),
        scratch_shapes=[
            pltpu.VMEM((2 * D_FF // W13_BLOCK, D_MODEL, W13_BLOCK), BF16),
            pltpu.VMEM((N_FF_CHUNKS, FF_CHUNK, D_MODEL), BF16),
            pltpu.VMEM((2, D_MODEL, W13_BLOCK), F32),
            pltpu.VMEM((2, FF_CHUNK, D_MODEL), F32),
            pltpu.SemaphoreType.DMA((2,)),
            pltpu.SemaphoreType.DMA((2,)),
        ],
        compiler_params=pltpu.CompilerParams(
            dimension_semantics=("arbitrary",), vmem_limit_bytes=VMEM_LIMIT),
        name="ffn_final" if final else "ffn",
    )(h2d, mod, ngain, w13, w2, fgain)


def _hgrn_constants():
    r = np.arange(CHUNK)
    prefix = (r[None, :] <= r[:, None]).astype(np.float32)
    pair = np.zeros((len(MATMUL_SIZES), CHUNK, CHUNK), np.float32)
    for i, m in enumerate(MATMUL_SIZES):
        blk = r // (2 * m)
        low = (r % (2 * m)) < m
        pair[i] = (blk[:, None] == blk[None, :]) & (~low)[:, None] & low[None, :]
    near = np.zeros((NEAR_ROWS, CHUNK, CHUNK), np.float32)
    for d in range(NEAR_ROWS):
        near[d] = (r[:, None] - r[None, :] == d) & ((r % NEAR_ROWS) >= d)[:, None]
    return np.concatenate([prefix, prefix], axis=1), np.concatenate([pair, near], axis=0)


def _level_log_decay(b, m):
    rows, width = b.shape
    if m >= SUBLANES:
        parts = []
        for r0 in range(0, rows, 2 * m):
            bref = b[r0 + m - 1:r0 + m, :]
            parts.append(bref - b[r0:r0 + m])
            parts.append(b[r0 + m:r0 + 2 * m] - bref)
        return jnp.concatenate(parts, axis=0)
    row = lax.broadcasted_iota(jnp.int32, b.shape, 0)

    def group_row(i):
        return jnp.concatenate(
            [jnp.broadcast_to(b[g + i:g + i + 1, :], (SUBLANES, width))
             for g in range(0, rows, SUBLANES)], axis=0)

    bref = group_row(m - 1)
    for blk in range(1, SUBLANES // (2 * m)):
        bref = jnp.where((row % SUBLANES) >= blk * 2 * m, group_row(blk * 2 * m + m - 1), bref)
    return jnp.where((row & m) != 0, b - bref, bref - b)


def _select_rows(m, upper_val, lower_val):
    rows = upper_val.shape[0]
    if m >= SUBLANES:
        parts = []
        for r0 in range(0, rows, m):
            src = upper_val if (r0 // m) % 2 == 1 else lower_val
            parts.append(src[r0:r0 + m])
        return jnp.concatenate(parts, axis=0)
    row = lax.broadcasted_iota(jnp.int32, upper_val.shape, 0)
    return jnp.where((row & m) != 0, upper_val, lower_val)


def _mixer_kernel(h_ref, mod_ref, ng_ref, win_hbm, wout_hbm, lbl_ref, gn_ref, lng_ref, og_ref,
                  wsp_ref, bsp_ref, prefix_ref, pair_ref, o_ref, st_ref, proj_ref, win_ref, wout_ref,
                  stage_in_ref, stage_out_ref, sem_in_ref, sem_out_ref, *, layer):
    @pl.when((pl.program_id(0) == 0) & (pl.program_id(1) == 0))
    def _():
        _stream_cast(win_ref.shape[0], _col_block(win_hbm, layer, WIN_BLOCK), win_ref,
                     stage_in_ref, sem_in_ref)
        _stream_cast(wout_ref.shape[0], _col_block(wout_hbm, layer, WOUT_BLOCK), wout_ref,
                     stage_out_ref, sem_out_ref)

    @pl.when(pl.program_id(1) == 0)
    def _():
        st_ref[...] = jnp.zeros_like(st_ref)

    sh = mod_ref[0, 3:4, :]
    sc = mod_ref[0, 4:5, :]
    gate = mod_ref[0, 5:6, :]

    def normalize(r0):
        y = _rms(h_ref[0, r0:r0 + SUB_ROWS, :], RMS_EPS) * ng_ref[...]
        return (y * (1.0 + sc) + sh).astype(BF16)

    def proj(r0, rows, c0, c1):
        return proj_ref[r0:r0 + rows, c0:c1]

    def project(r0, y, c0, c1):
        for c in range(c0, c1, WIN_BLOCK):
            proj_ref[r0:r0 + SUB_ROWS, c:c + WIN_BLOCK] = jnp.dot(
                y, _cols(win_ref, c, WIN_BLOCK), preferred_element_type=F32)

    def out_project(r0, cat, c0, c1):
        mix = jnp.dot(cat, _cols(wout_ref, c0, c1 - c0), preferred_element_type=F32)
        o_ref[0, r0:r0 + SUB_ROWS, c0:c1] = h_ref[0, r0:r0 + SUB_ROWS, c0:c1] + gate[:, c0:c1] * mix

    lg = lbl_ref[...]
    ex = jnp.exp(lg - jnp.max(lg, axis=0, keepdims=True))
    prob = ex / jnp.sum(ex, axis=0, keepdims=True)
    lb = jnp.zeros((1, HG_WIDTH), F32)
    for i in range(1, layer + 1):
        lb = lb + prob[i:i + 1, :]

    r2 = lax.broadcasted_iota(jnp.int32, (GM_BLOCK, GM_BLOCK), 0)
    c2 = lax.broadcasted_iota(jnp.int32, (GM_BLOCK, GM_BLOCK), 1)
    causal = (r2 // CHUNK) >= (c2 // CHUNK)
    w_sp = [jnp.where(causal, wsp_ref[hd], 0.0).astype(BF16) for hd in range(GM_HEADS)]

    def gm_mix(r0, n_blocks, head_ids):
        res = [[] for _ in range(n_blocks)]
        for hd in head_ids:
            cu = 4 * HG_WIDTH + hd * GM_DH
            cv = cu + GM_WIDTH
            us, vns = [], []
            for blk in range(n_blocks):
                rb = r0 + blk * GM_BLOCK
                us.append(jax.nn.gelu(proj(rb, GM_BLOCK, cu, cu + GM_DH)))
                v = jax.nn.gelu(proj(rb, GM_BLOCK, cv, cv + GM_DH))
                vc = v - jnp.mean(v, axis=-1, keepdims=True)
                var = jnp.mean(vc * vc, axis=-1, keepdims=True)
                vn = vc * lax.rsqrt(var + LN_EPS) * lng_ref[:, hd * GM_DH:(hd + 1) * GM_DH]
                vns.append(vn.astype(BF16))
            mixed = jnp.dot(w_sp[hd], jnp.concatenate(vns, axis=1), preferred_element_type=F32)
            for blk in range(n_blocks):
                res[blk].append((us[blk], mixed[:, blk * GM_DH:(blk + 1) * GM_DH] + bsp_ref[hd]))
        return res

    def gm_gate(res):
        outs = []
        for hd, (u, mixed) in enumerate(res):
            z = _rms(u * mixed, RMS_EPS) * og_ref[:, hd * GM_DH:(hd + 1) * GM_DH]
            outs.append(z.astype(BF16))
        return jnp.concatenate(outs, axis=1)

    def hg_gates(r0):
        q = proj(r0, CHUNK, 0, HG_WIDTH)
        f_raw = proj(r0, CHUNK, HG_WIDTH, 2 * HG_WIDTH)
        vb = proj(r0, CHUNK, 2 * HG_WIDTH, 3 * HG_WIDTH).astype(BF16)
        g = proj(r0, CHUNK, 3 * HG_WIDTH, 4 * HG_WIDTH)
        half_g = 0.5 * g
        g_act = half_g + half_g * jnp.tanh(half_g)
        f = lb + (1.0 - lb) * jax.nn.sigmoid(f_raw)
        log_f = jnp.log(f) * LOG2_E
        k = 1.0 - f
        hi = log_f.astype(BF16)
        mid = (log_f - hi.astype(F32)).astype(BF16)
        b = jnp.dot(prefix_ref[...], jnp.concatenate([hi, mid], axis=0),
                    preferred_element_type=F32)
        return q, k, f, vb, g_act, b

    def hg_scores(gates):
        q, k, f, vb_all, g_act, b = gates
        decay_in = jnp.exp2(b)
        decay_out = jnp.exp2(b[CHUNK - 1:CHUNK, :] - b)
        q_in = (q * decay_in).astype(BF16)
        k_out = (k * decay_out).astype(BF16)
        heads = [slice(hd * HG_DK, (hd + 1) * HG_DK) for hd in range(HG_HEADS)]
        def earlier(x, d):
            grouped = x.reshape(x.shape[0] // SUBLANES, SUBLANES, x.shape[1])
            return pltpu.roll(grouped, d, 1).reshape(x.shape)

        near = [q * k]
        decay = None
        for d in range(1, NEAR_ROWS):
            decay = f if d == 1 else decay * earlier(f, d - 1)
            near.append(q * earlier(k, d) * decay)
        levels = [(_select_rows(m, q, k) * jnp.exp2(_level_log_decay(b, m))).astype(BF16)
                  for m in MATMUL_SIZES]
        dn = (((1,), (1,)), ((), ()))
        per_head = []
        for cs in heads:
            att = sum(jnp.sum(prod[:, cs], axis=-1, keepdims=True) * pair_ref[len(MATMUL_SIZES) + d]
                      for d, prod in enumerate(near))
            for i, level in enumerate(levels):
                mh = level[:, cs]
                att = att + lax.dot_general(mh, mh, dn, preferred_element_type=F32) * pair_ref[i]
            vb = vb_all[:, cs]
            inc = lax.dot_general(vb, k_out[:, cs], (((0,), (0,)), ((), ())),
                                  preferred_element_type=F32)
            per_head.append((att.astype(BF16), vb, q_in[:, cs], inc, decay_in[CHUNK - 1:CHUNK, cs],
                             g_act[:, cs]))
        return per_head

    def hg_output(per_head):
        dn = (((1,), (1,)), ((), ()))
        outs = []
        for hd, (att, vb, q_in, inc, decay_last, g_act) in enumerate(per_head):
            cs = slice(hd * HG_DK, (hd + 1) * HG_DK)
            st = st_ref[hd]
            o = (jnp.dot(att, vb, preferred_element_type=F32)
                 + lax.dot_general(q_in, st.astype(BF16), dn, preferred_element_type=F32))
            st_ref[hd] = st * decay_last + inc
            o = _rms(o, RMS_EPS) * gn_ref[:, cs] * g_act
            outs.append(o.astype(BF16))
        return jnp.concatenate(outs, axis=1)

    n_sub = MIX_ROWS // SUB_ROWS
    n_chunks = MIX_ROWS // CHUNK
    chunks_per_sub = SUB_ROWS // CHUNK
    chunks_per_blk = GM_BLOCK // CHUNK
    blocks_per_sub = SUB_ROWS // GM_BLOCK
    in_piece = IN_COLS // chunks_per_sub
    out_piece = D_MODEL // chunks_per_sub
    lag_out = 2

    y = [normalize(s * SUB_ROWS) for s in range(n_sub)]
    project(0, y[0], 0, IN_COLS)
    gates, scores, gm_state, hg_out, gm_out, cat = {}, {}, {}, {}, {}, {}
    def big_projections(slot):
        s_next = slot // chunks_per_sub + 1
        if s_next < n_sub:
            p = slot % chunks_per_sub
            project(s_next * SUB_ROWS, y[s_next], p * in_piece, (p + 1) * in_piece)
        done = slot - lag_out - chunks_per_sub
        if done >= 0 and done // chunks_per_sub < n_sub:
            s_done, p = divmod(done, chunks_per_sub)
            if p == 0:
                c0 = s_done * chunks_per_sub
                hg_part = jnp.concatenate([hg_out.pop(c0 + i) for i in range(chunks_per_sub)], axis=0)
                gm_part = jnp.concatenate(
                    [gm_out.pop(c0 + i) for i in range(0, chunks_per_sub, chunks_per_blk)], axis=0)
                cat[s_done] = jnp.concatenate([hg_part, gm_part], axis=1)
            out_project(s_done * SUB_ROWS, cat[s_done], p * out_piece, (p + 1) * out_piece)

    for slot in range(n_chunks + lag_out + chunks_per_sub):
        big_projections(slot)
        if 0 <= slot - lag_out < n_chunks:
            hg_out[slot - lag_out] = hg_output(scores.pop(slot - lag_out))
        if 0 <= slot - 1 < n_chunks:
            scores[slot - 1] = hg_scores(gates.pop(slot - 1))
        if slot < n_chunks:
            gates[slot] = hg_gates(slot * CHUNK)
            p = slot % chunks_per_sub
            if p < 2:
                head_ids = range(p * GM_HEADS // 2, (p + 1) * GM_HEADS // 2)
                for i, res in enumerate(gm_mix((slot - p) * CHUNK, blocks_per_sub, head_ids)):
                    gm_state.setdefault(slot - p + i * chunks_per_blk, []).extend(res)
            else:
                key = slot - p + (p - 2) * chunks_per_blk
                gm_out[key] = gm_gate(gm_state.pop(key))


def _mixer_call(h, mod, ngain, w_in, w_out, lb_logits, gnorm, ln_gain, out_gain, w_sp, b_sp,
                prefix_c, pair_c, *, layer):
    batch = h.shape[0]
    depth = lb_logits.shape[0]
    kern = functools.partial(_mixer_kernel, layer=layer)
    return pl.pallas_call(
        kern,
        grid=(batch, SEQ // MIX_ROWS),
        in_specs=[
            pl.BlockSpec((1, MIX_ROWS, D_MODEL), lambda b, j: (b, j, 0)),
            pl.BlockSpec((1, N_MOD, D_MODEL), lambda b, j: (b, 0, 0)),
            _resident((1, D_MODEL)),
            pl.BlockSpec(memory_space=pl.ANY),
            pl.BlockSpec(memory_space=pl.ANY),
            _resident((depth, HG_WIDTH)),
            _resident((1, HG_WIDTH)),
            _resident((1, GM_WIDTH)),
            _resident((1, GM_WIDTH)),
            _resident((GM_HEADS, GM_BLOCK, GM_BLOCK)),
            _resident((GM_HEADS, GM_BLOCK, 1)),
            _resident(prefix_c.shape),
            _resident(pair_c.shape),
        ],
        out_specs=pl.BlockSpec((1, MIX_ROWS, D_MODEL), lambda b, j: (b, j, 0)),
        out_shape=jax.ShapeDtypeStruct(h.shape, F32),
        scratch_shapes=[pltpu.VMEM((HG_HEADS, HG_DK, HG_DK), F32),
                        pltpu.VMEM((MIX_ROWS, IN_COLS), F32),
                        pltpu.VMEM((IN_COLS // WIN_BLOCK, D_MODEL, WIN_BLOCK), BF16),
                        pltpu.VMEM((D_MODEL // WOUT_BLOCK, D_MODEL, WOUT_BLOCK), BF16),
                        pltpu.VMEM((2, D_MODEL, WIN_BLOCK), F32),
                        pltpu.VMEM((2, D_MODEL, WOUT_BLOCK), F32),
                        pltpu.SemaphoreType.DMA((2,)),
                        pltpu.SemaphoreType.DMA((2,))],
        compiler_params=pltpu.CompilerParams(
            dimension_semantics=("arbitrary", "arbitrary"), vmem_limit_bytes=VMEM_LIMIT),
        name="mixer",
    )(h, mod, ngain, w_in, w_out, lb_logits, gnorm, ln_gain, out_gain, w_sp, b_sp, prefix_c, pair_c)


def kernel(x, c, w_ada, b_ada, norm_gain, ffn1_w13, ffn1_w2, w_in, hg_lb_logits, hg_gnorm,
           gm_ln_gain, gm_w_spatial, gm_b_spatial, gm_out_gain, w_out, ffn2_w13, ffn2_w2,
           final_gain):
    batch, seq, _ = x.shape
    depth = w_ada.shape[0]
    prefix_np, pair_np = _hgrn_constants()
    prefix_c = jnp.asarray(prefix_np, BF16)
    pair_c = jnp.asarray(pair_np, F32)
    fgain = final_gain.reshape(1, D_MODEL)

    mod_all = _ada_call(c, w_ada, b_ada).reshape(depth, batch, N_MOD, D_MODEL)
    h = x
    for l in range(depth):
        mod = mod_all[l]
        h = _ffn_call(h.reshape(batch * seq, D_MODEL), mod, norm_gain[l, 0:1], ffn1_w13, ffn1_w2,
                      fgain, layer=l, mod_base=0, final=False).reshape(batch, seq, D_MODEL)
        h = _mixer_call(h, mod, norm_gain[l, 1:2], w_in, w_out,
                        hg_lb_logits, hg_gnorm[l:l + 1], gm_ln_gain[l:l + 1], gm_out_gain[l:l + 1],
                        gm_w_spatial[l], gm_b_spatial[l].reshape(GM_HEADS, GM_BLOCK, 1),
                        prefix_c, pair_c, layer=l)
        h = _ffn_call(h.reshape(batch * seq, D_MODEL), mod, norm_gain[l, 2:3], ffn2_w13, ffn2_w2,
                      fgain, layer=l, mod_base=6,
                      final=(l == depth - 1)).reshape(batch, seq, D_MODEL)
    return h
```

```python
import functools

import numpy as np
import jax
import jax.numpy as jnp
from jax import lax
from jax.experimental import pallas as pl
from jax.experimental.pallas import tpu as pltpu

D_MODEL = 1024
SEQ = 2048
CHUNK = 64
HG_WIDTH = 512
HG_HEADS = 4
HG_DK = 128
GM_WIDTH = 512
GM_HEADS = 4
GM_DH = 128
GM_BLOCK = 128
D_FF = 2816
N_MOD = 9
IN_COLS = 4 * HG_WIDTH + 2 * GM_WIDTH
RMS_EPS = 1e-6
LN_EPS = 1e-5

FF_CHUNK = 256
N_FF_CHUNKS = D_FF // FF_CHUNK
W13_BLOCK = 2 * FF_CHUNK
WIN_BLOCK = 768
WOUT_BLOCK = 256
FFN_ROWS = 1024
MIX_ROWS = 512
SUB_ROWS = 256
ADA_COLS = 2304
NEAR_ROWS = 4
MATMUL_SIZES = (32, 16, 8, 4)
SUBLANES = 8
VMEM_LIMIT = 56 * 1024 * 1024
LOG2_E = 1.4426950408889634

F32 = jnp.float32
BF16 = jnp.bfloat16


def _rms(x, eps):
    return x * lax.rsqrt(jnp.mean(x * x, axis=-1, keepdims=True) + eps)


def _resident(shape):
    return pl.BlockSpec(shape, lambda *_: (0,) * len(shape), pipeline_mode=pl.Buffered(1))


def _ada_kernel(c_ref, w_ref, b_ref, o_ref):
    c = c_ref[...]
    ca = c * jax.nn.sigmoid(c)
    batch = ca.shape[0]
    ca_hi = ca.astype(BF16)
    ca_lo = (ca - ca_hi.astype(F32)).astype(BF16)
    w = w_ref[0]
    w_hi = w.astype(BF16)
    w_lo = (w - w_hi.astype(F32)).astype(BF16)
    p = jnp.dot(jnp.concatenate([ca_hi, ca_lo], axis=0), w_hi, preferred_element_type=F32)
    q = jnp.dot(ca_hi, w_lo, preferred_element_type=F32)
    o_ref[0] = p[:batch] + p[batch:] + q + b_ref[0]


def _ada_call(c, w_ada, b_ada):
    depth, _, n_out = w_ada.shape
    batch = c.shape[0]
    return pl.pallas_call(
        _ada_kernel,
        grid=(depth, n_out // ADA_COLS),
        in_specs=[
            pl.BlockSpec((batch, D_MODEL), lambda l, j: (0, 0)),
            pl.BlockSpec((1, D_MODEL, ADA_COLS), lambda l, j: (l, 0, j)),
            pl.BlockSpec((1, 1, ADA_COLS), lambda l, j: (l, 0, j)),
        ],
        out_specs=pl.BlockSpec((1, batch, ADA_COLS), lambda l, j: (l, 0, j)),
        out_shape=jax.ShapeDtypeStruct((depth, batch, n_out), F32),
        compiler_params=pltpu.CompilerParams(
            dimension_semantics=("arbitrary", "arbitrary"), vmem_limit_bytes=VMEM_LIMIT),
        name="ada",
    )(c, w_ada, b_ada.reshape(depth, 1, n_out))


def _stream_cast(n_blocks, src_block, dst_ref, stage_ref, sem_ref):
    def copy(i, slot):
        return pltpu.make_async_copy(src_block(i), stage_ref.at[slot], sem_ref.at[slot])

    copy(0, 0).start()

    def body(i, carry):
        slot = i % 2

        @pl.when(i + 1 < n_blocks)
        def _():
            copy(i + 1, 1 - slot).start()

        copy(i, slot).wait()
        dst_ref[i] = stage_ref[slot].astype(BF16)
        return carry

    lax.fori_loop(0, n_blocks, body, 0)


def _col_block(w_hbm, layer, width):
    return lambda i: w_hbm.at[layer, :, pl.ds(pl.multiple_of(i * width, width), width)]


def _row_block(w_hbm, layer, height):
    return lambda i: w_hbm.at[layer, pl.ds(pl.multiple_of(i * height, height), height), :]


def _cols(w_ref, c0, width):
    block_width = w_ref.shape[2]
    blk, off = divmod(c0, block_width)
    assert off + width <= block_width
    return w_ref[blk, :, off:off + width]


def _ffn_kernel(h_ref, mod_ref, ng_ref, w13_hbm, w2_hbm, fg_ref, o_ref, w13_ref, w2_ref,
                stage13_ref, stage2_ref, sem13_ref, sem2_ref, *, layer, mod_base, final):
    @pl.when(pl.program_id(0) == 0)
    def _():
        _stream_cast(w13_ref.shape[0], _col_block(w13_hbm, layer, W13_BLOCK), w13_ref,
                     stage13_ref, sem13_ref)
        _stream_cast(w2_ref.shape[0], _row_block(w2_hbm, layer, FF_CHUNK), w2_ref,
                     stage2_ref, sem2_ref)

    h = h_ref[...]
    sh = mod_ref[0, mod_base:mod_base + 1, :]
    sc = mod_ref[0, mod_base + 1:mod_base + 2, :]
    gate = mod_ref[0, mod_base + 2:mod_base + 3, :]
    y = _rms(h, RMS_EPS) * ng_ref[...]
    yb = (y * (1.0 + sc) + sh).astype(BF16)
    acc = None
    for j in range(N_FF_CHUNKS):
        c0 = j * FF_CHUNK
        a = jnp.dot(yb, _cols(w13_ref, c0, FF_CHUNK), preferred_element_type=F32)
        b = jnp.dot(yb, _cols(w13_ref, D_FF + c0, FF_CHUNK), preferred_element_type=F32)
        hm = (a * jax.nn.sigmoid(a) * b).astype(BF16)
        d = jnp.dot(hm, w2_ref[j], preferred_element_type=F32)
        acc = d if acc is None else acc + d
    res = h_ref[...] + 0.5 * gate * acc
    if final:
        res = _rms(res, RMS_EPS) * fg_ref[...]
    o_ref[...] = res


def _ffn_call(h2d, mod, ngain, w13, w2, fgain, *, layer, mod_base, final):
    n_tok = h2d.shape[0]
    tiles_per_seq = SEQ // FFN_ROWS
    kern = functools.partial(_ffn_kernel, layer=layer, mod_base=mod_base, final=final)
    return pl.pallas_call(
        kern,
        grid=(n_tok // FFN_ROWS,),
        in_specs=[
            pl.BlockSpec((FFN_ROWS, D_MODEL), lambda i: (i, 0)),
            pl.BlockSpec((1, N_MOD, D_MODEL), lambda i: (i // tiles_per_seq, 0, 0)),
            _resident((1, D_MODEL)),
            pl.BlockSpec(memory_space=pl.ANY),
            pl.BlockSpec(memory_space=pl.ANY),
            _resident((1, D_MODEL)),
        ],
        out_specs=pl.BlockSpec((FFN_ROWS, D_MODEL), lambda i: (i, 0)),
        out_shape=jax.ShapeDtypeStruct(h2d.shape, F32),
        scratch_shapes=[
            pltpu.VMEM((2 * D_FF // W13_BLOCK, D_MODEL, W13_BLOCK), BF16),
            pltpu.VMEM((N_FF_CHUNKS, FF_CHUNK, D_MODEL), BF16),
            pltpu.VMEM((2, D_MODEL, W13_BLOCK), F32),
            pltpu.VMEM((2, FF_CHUNK, D_MODEL), F32),
            pltpu.SemaphoreType.DMA((2,)),
            pltpu.SemaphoreType.DMA((2,)),
        ],
        compiler_params=pltpu.CompilerParams(
            dimension_semantics=("arbitrary",), vmem_limit_bytes=VMEM_LIMIT),
        name="ffn_final" if final else "ffn",
    )(h2d, mod, ngain, w13, w2, fgain)


def _hgrn_constants():
    r = np.arange(CHUNK)
    prefix = (r[None, :] <= r[:, None]).astype(np.float32)
    pair = np.zeros((len(MATMUL_SIZES), CHUNK, CHUNK), np.float32)
    for i, m in enumerate(MATMUL_SIZES):
        blk = r // (2 * m)
        low = (r % (2 * m)) < m
        pair[i] = (blk[:, None] == blk[None, :]) & (~low)[:, None] & low[None, :]
    near = np.zeros((NEAR_ROWS, CHUNK, CHUNK), np.float32)
    for d in range(NEAR_ROWS):
        near[d] = (r[:, None] - r[None, :] == d) & ((r % NEAR_ROWS) >= d)[:, None]
    return np.concatenate([prefix, prefix], axis=1), np.concatenate([pair, near], axis=0)


def _level_log_decay(b, m):
    rows, width = b.shape
    if m >= SUBLANES:
        parts = []
        for r0 in range(0, rows, 2 * m):
            bref = b[r0 + m - 1:r0 + m, :]
            parts.append(bref - b[r0:r0 + m])
            parts.append(b[r0 + m:r0 + 2 * m] - bref)
        return jnp.concatenate(parts, axis=0)
    row = lax.broadcasted_iota(jnp.int32, b.shape, 0)

    def group_row(i):
        return jnp.concatenate(
            [jnp.broadcast_to(b[g + i:g + i + 1, :], (SUBLANES, width))
             for g in range(0, rows, SUBLANES)], axis=0)

    bref = group_row(m - 1)
    for blk in range(1, SUBLANES // (2 * m)):
        bref = jnp.where((row % SUBLANES) >= blk * 2 * m, group_row(blk * 2 * m + m - 1), bref)
    return jnp.where((row & m) != 0, b - bref, bref - b)


def _select_rows(m, upper_val, lower_val):
    rows = upper_val.shape[0]
    if m >= SUBLANES:
        parts = []
        for r0 in range(0, rows, m):
            src = upper_val if (r0 // m) % 2 == 1 else lower_val
            parts.append(src[r0:r0 + m])
        return jnp.concatenate(parts, axis=0)
    row = lax.broadcasted_iota(jnp.int32, upper_val.shape, 0)
    return jnp.where((row & m) != 0, upper_val, lower_val)


def _mixer_kernel(h_ref, mod_ref, ng_ref, win_hbm, wout_hbm, lbl_ref, gn_ref, lng_ref, og_ref,
                  wsp_ref, bsp_ref, prefix_ref, pair_ref, o_ref, st_ref, proj_ref, win_ref, wout_ref,
                  stage_in_ref, stage_out_ref, sem_in_ref, sem_out_ref, *, layer):
    @pl.when((pl.program_id(0) == 0) & (pl.program_id(1) == 0))
    def _():
        _stream_cast(win_ref.shape[0], _col_block(win_hbm, layer, WIN_BLOCK), win_ref,
                     stage_in_ref, sem_in_ref)
        _stream_cast(wout_ref.shape[0], _col_block(wout_hbm, layer, WOUT_BLOCK), wout_ref,
                     stage_out_ref, sem_out_ref)

    @pl.when(pl.program_id(1) == 0)
    def _():
        st_ref[...] = jnp.zeros_like(st_ref)

    sh = mod_ref[0, 3:4, :]
    sc = mod_ref[0, 4:5, :]
    gate = mod_ref[0, 5:6, :]

    def normalize(r0):
        y = _rms(h_ref[0, r0:r0 + SUB_ROWS, :], RMS_EPS) * ng_ref[...]
        return (y * (1.0 + sc) + sh).astype(BF16)

    def proj(r0, rows, c0, c1):
        return proj_ref[r0:r0 + rows, c0:c1]

    def project(r0, y, c0, c1):
        for c in range(c0, c1, WIN_BLOCK):
            proj_ref[r0:r0 + SUB_ROWS, c:c + WIN_BLOCK] = jnp.dot(
                y, _cols(win_ref, c, WIN_BLOCK), preferred_element_type=F32)

    def out_project(r0, cat, c0, c1):
        mix = jnp.dot(cat, _cols(wout_ref, c0, c1 - c0), preferred_element_type=F32)
        o_ref[0, r0:r0 + SUB_ROWS, c0:c1] = h_ref[0, r0:r0 + SUB_ROWS, c0:c1] + gate[:, c0:c1] * mix

    lg = lbl_ref[...]
    ex = jnp.exp(lg - jnp.max(lg, axis=0, keepdims=True))
    prob = ex / jnp.sum(ex, axis=0, keepdims=True)
    lb = jnp.zeros((1, HG_WIDTH), F32)
    for i in range(1, layer + 1):
        lb = lb + prob[i:i + 1, :]

    r2 = lax.broadcasted_iota(jnp.int32, (GM_BLOCK, GM_BLOCK), 0)
    c2 = lax.broadcasted_iota(jnp.int32, (GM_BLOCK, GM_BLOCK), 1)
    causal = (r2 // CHUNK) >= (c2 // CHUNK)
    w_sp = [jnp.where(causal, wsp_ref[hd], 0.0).astype(BF16) for hd in range(GM_HEADS)]

    def gm_mix(r0, n_blocks, head_ids):
        res = [[] for _ in range(n_blocks)]
        for hd in head_ids:
            cu = 4 * HG_WIDTH + hd * GM_DH
            cv = cu + GM_WIDTH
            us, vns = [], []
            for blk in range(n_blocks):
                rb = r0 + blk * GM_BLOCK
                us.append(jax.nn.gelu(proj(rb, GM_BLOCK, cu, cu + GM_DH)))
                v = jax.nn.gelu(proj(rb, GM_BLOCK, cv, cv + GM_DH))
                vc = v - jnp.mean(v, axis=-1, keepdims=True)
                var = jnp.mean(vc * vc, axis=-1, keepdims=True)
                vn = vc * lax.rsqrt(var + LN_EPS) * lng_ref[:, hd * GM_DH:(hd + 1) * GM_DH]
                vns.append(vn.astype(BF16))
            mixed = jnp.dot(w_sp[hd], jnp.concatenate(vns, axis=1), preferred_element_type=F32)
            for blk in range(n_blocks):
                res[blk].append((us[blk], mixed[:, blk * GM_DH:(blk + 1) * GM_DH] + bsp_ref[hd]))
        return res

    def gm_gate(res):
        outs = []
        for hd, (u, mixed) in enumerate(res):
            z = _rms(u * mixed, RMS_EPS) * og_ref[:, hd * GM_DH:(hd + 1) * GM_DH]
            outs.append(z.astype(BF16))
        return jnp.concatenate(outs, axis=1)

    def hg_gates(r0):
        q = proj(r0, CHUNK, 0, HG_WIDTH)
        f_raw = proj(r0, CHUNK, HG_WIDTH, 2 * HG_WIDTH)
        f = lb + (1.0 - lb) * jax.nn.sigmoid(f_raw)
        log_f = jnp.log(f) * LOG2_E
        k = 1.0 - f
        hi = log_f.astype(BF16)
        mid = (log_f - hi.astype(F32)).astype(BF16)
        b = jnp.dot(prefix_ref[...], jnp.concatenate([hi, mid], axis=0),
                    preferred_element_type=F32)
        return r0, q, k, f, b

    def hg_scores(gates):
        r0, q_all, k_all, f_all, b_all = gates

        def earlier(x, d):
            grouped = x.reshape(x.shape[0] // SUBLANES, SUBLANES, x.shape[1])
            return pltpu.roll(grouped, d, 1).reshape(x.shape)

        dn = (((1,), (1,)), ((), ()))
        per_head = []
        for hd in range(HG_HEADS):
            cs = slice(hd * HG_DK, (hd + 1) * HG_DK)
            q, k, f, b = q_all[:, cs], k_all[:, cs], f_all[:, cs], b_all[:, cs]
            decay_in = jnp.exp2(b)
            decay_out = jnp.exp2(b[CHUNK - 1:CHUNK, :] - b)
            att = jnp.sum(q * k, axis=-1, keepdims=True) * pair_ref[len(MATMUL_SIZES)]
            decay = None
            for d in range(1, NEAR_ROWS):
                decay = f if d == 1 else decay * earlier(f, d - 1)
                att = att + (jnp.sum(q * earlier(k, d) * decay, axis=-1, keepdims=True)
                             * pair_ref[len(MATMUL_SIZES) + d])
            for i, m in enumerate(MATMUL_SIZES):
                mh = (_select_rows(m, q, k) * jnp.exp2(_level_log_decay(b, m))).astype(BF16)
                att = att + lax.dot_general(mh, mh, dn, preferred_element_type=F32) * pair_ref[i]
            vb = proj(r0, CHUNK, 2 * HG_WIDTH + hd * HG_DK, 2 * HG_WIDTH + (hd + 1) * HG_DK)
            per_head.append((att.astype(BF16), vb.astype(BF16), (q * decay_in).astype(BF16),
                             (k * decay_out).astype(BF16), decay_in[CHUNK - 1:CHUNK, :]))
        return r0, per_head

    def hg_output(scored):
        r0, per_head = scored
        dn = (((1,), (1,)), ((), ()))
        outs = []
        for hd, (att, vb, q_in, k_out, decay_last) in enumerate(per_head):
            cs = slice(hd * HG_DK, (hd + 1) * HG_DK)
            st = st_ref[hd]
            o = (jnp.dot(att, vb, preferred_element_type=F32)
                 + lax.dot_general(q_in, st.astype(BF16), dn, preferred_element_type=F32))
            inc = lax.dot_general(vb, k_out, (((0,), (0,)), ((), ())),
                                  preferred_element_type=F32)
            st_ref[hd] = st * decay_last + inc
            half_g = 0.5 * proj(r0, CHUNK, 3 * HG_WIDTH + hd * HG_DK, 3 * HG_WIDTH + (hd + 1) * HG_DK)
            g_act = half_g + half_g * jnp.tanh(half_g)
            o = _rms(o, RMS_EPS) * gn_ref[:, cs] * g_act
            outs.append(o.astype(BF16))
        return jnp.concatenate(outs, axis=1)

    n_sub = MIX_ROWS // SUB_ROWS
    n_chunks = MIX_ROWS // CHUNK
    chunks_per_sub = SUB_ROWS // CHUNK
    chunks_per_blk = GM_BLOCK // CHUNK
    blocks_per_sub = SUB_ROWS // GM_BLOCK
    in_piece = IN_COLS // chunks_per_sub
    out_piece = D_MODEL // chunks_per_sub
    lag_out = 2

    y = [normalize(s * SUB_ROWS) for s in range(n_sub)]
    project(0, y[0], 0, IN_COLS)
    gates, scores, gm_state, hg_out, gm_out, cat = {}, {}, {}, {}, {}, {}
    def big_projections(slot):
        s_next = slot // chunks_per_sub + 1
        if s_next < n_sub:
            p = slot % chunks_per_sub
            project(s_next * SUB_ROWS, y[s_next], p * in_piece, (p + 1) * in_piece)
        done = slot - lag_out - chunks_per_sub
        if done >= 0 and done // chunks_per_sub < n_sub:
            s_done, p = divmod(done, chunks_per_sub)
            if p == 0:
                c0 = s_done * chunks_per_sub
                hg_part = jnp.concatenate([hg_out.pop(c0 + i) for i in range(chunks_per_sub)], axis=0)
                gm_part = jnp.concatenate(
                    [gm_out.pop(c0 + i) for i in range(0, chunks_per_sub, chunks_per_blk)], axis=0)
                cat[s_done] = jnp.concatenate([hg_part, gm_part], axis=1)
            out_project(s_done * SUB_ROWS, cat[s_done], p * out_piece, (p + 1) * out_piece)

    for slot in range(n_chunks + lag_out + chunks_per_sub):
        big_projections(slot)
        if 0 <= slot - lag_out < n_chunks:
            hg_out[slot - lag_out] = hg_output(scores.pop(slot - lag_out))
        if 0 <= slot - 1 < n_chunks:
            scores[slot - 1] = hg_scores(gates.pop(slot - 1))
        if slot < n_chunks:
            gates[slot] = hg_gates(slot * CHUNK)
            p = slot % chunks_per_sub
            if p < 2:
                head_ids = range(p * GM_HEADS // 2, (p + 1) * GM_HEADS // 2)
                for i, res in enumerate(gm_mix((slot - p) * CHUNK, blocks_per_sub, head_ids)):
                    gm_state.setdefault(slot - p + i * chunks_per_blk, []).extend(res)
            else:
                key = slot - p + (p - 2) * chunks_per_blk
                gm_out[key] = gm_gate(gm_state.pop(key))


def _mixer_call(h, mod, ngain, w_in, w_out, lb_logits, gnorm, ln_gain, out_gain, w_sp, b_sp,
                prefix_c, pair_c, *, layer):
    batch = h.shape[0]
    depth = lb_logits.shape[0]
    kern = functools.partial(_mixer_kernel, layer=layer)
    return pl.pallas_call(
        kern,
        grid=(batch, SEQ // MIX_ROWS),
        in_specs=[
            pl.BlockSpec((1, MIX_ROWS, D_MODEL), lambda b, j: (b, j, 0)),
            pl.BlockSpec((1, N_MOD, D_MODEL), lambda b, j: (b, 0, 0)),
            _resident((1, D_MODEL)),
            pl.BlockSpec(memory_space=pl.ANY),
            pl.BlockSpec(memory_space=pl.ANY),
            _resident((depth, HG_WIDTH)),
            _resident((1, HG_WIDTH)),
            _resident((1, GM_WIDTH)),
            _resident((1, GM_WIDTH)),
            _resident((GM_HEADS, GM_BLOCK, GM_BLOCK)),
            _resident((GM_HEADS, GM_BLOCK, 1)),
            _resident(prefix_c.shape),
            _resident(pair_c.shape),
        ],
        out_specs=pl.BlockSpec((1, MIX_ROWS, D_MODEL), lambda b, j: (b, j, 0)),
        out_shape=jax.ShapeDtypeStruct(h.shape, F32),
        scratch_shapes=[pltpu.VMEM((HG_HEADS, HG_DK, HG_DK), F32),
                        pltpu.VMEM((MIX_ROWS, IN_COLS), F32),
                        pltpu.VMEM((IN_COLS // WIN_BLOCK, D_MODEL, WIN_BLOCK), BF16),
                        pltpu.VMEM((D_MODEL // WOUT_BLOCK, D_MODEL, WOUT_BLOCK), BF16),
                        pltpu.VMEM((2, D_MODEL, WIN_BLOCK), F32),
                        pltpu.VMEM((2, D_MODEL, WOUT_BLOCK), F32),
                        pltpu.SemaphoreType.DMA((2,)),
                        pltpu.SemaphoreType.DMA((2,))],
        compiler_params=pltpu.CompilerParams(
            dimension_semantics=("arbitrary", "arbitrary"), vmem_limit_bytes=VMEM_LIMIT),
        name="mixer",
    )(h, mod, ngain, w_in, w_out, lb_logits, gnorm, ln_gain, out_gain, w_sp, b_sp, prefix_c, pair_c)


def kernel(x, c, w_ada, b_ada, norm_gain, ffn1_w13, ffn1_w2, w_in, hg_lb_logits, hg_gnorm,
           gm_ln_gain, gm_w_spatial, gm_b_spatial, gm_out_gain, w_out, ffn2_w13, ffn2_w2,
           final_gain):
    batch, seq, _ = x.shape
    depth = w_ada.shape[0]
    prefix_np, pair_np = _hgrn_constants()
    prefix_c = jnp.asarray(prefix_np, BF16)
    pair_c = jnp.asarray(pair_np, F32)
    fgain = final_gain.reshape(1, D_MODEL)

    mod_all = _ada_call(c, w_ada, b_ada).reshape(depth, batch, N_MOD, D_MODEL)
    h = x
    for l in range(depth):
        mod = mod_all[l]
        h = _ffn_call(h.reshape(batch * seq, D_MODEL), mod, norm_gain[l, 0:1], ffn1_w13, ffn1_w2,
                      fgain, layer=l, mod_base=0, final=False).reshape(batch, seq, D_MODEL)
        h = _mixer_call(h, mod, norm_gain[l, 1:2], w_in, w_out,
                        hg_lb_logits, hg_gnorm[l:l + 1], gm_ln_gain[l:l + 1], gm_out_gain[l:l + 1],
                        gm_w_spatial[l], gm_b_spatial[l].reshape(GM_HEADS, GM_BLOCK, 1),
                        prefix_c, pair_c, layer=l)
        h = _ffn_call(h.reshape(batch * seq, D_MODEL), mod, norm_gain[l, 2:3], ffn2_w13, ffn2_w2,
                      fgain, layer=l, mod_base=6,
                      final=(l == depth - 1)).reshape(batch, seq, D_MODEL)
    return h
```

```python
import functools

import numpy as np
import jax
import jax.numpy as jnp
from jax import lax
from jax.experimental import pallas as pl
from jax.experimental.pallas import tpu as pltpu

D_MODEL = 1024
SEQ = 2048
CHUNK = 64
HG_WIDTH = 512
HG_HEADS = 4
HG_DK = 128
GM_WIDTH = 512
GM_HEADS = 4
GM_DH = 128
GM_BLOCK = 128
D_FF = 2816
N_MOD = 9
IN_COLS = 4 * HG_WIDTH + 2 * GM_WIDTH
RMS_EPS = 1e-6
LN_EPS = 1e-5

FF_CHUNK = 256
N_FF_CHUNKS = D_FF // FF_CHUNK
W13_BLOCK = 2 * FF_CHUNK
WIN_BLOCK = 768
WOUT_BLOCK = 256
FFN_ROWS = 1024
MIX_ROWS = 512
SUB_ROWS = 256
ADA_COLS = 2304
NEAR_ROWS = 4
MATMUL_SIZES = (32, 16, 8, 4)
SUBLANES = 8
VMEM_LIMIT = 56 * 1024 * 1024
LOG2_E = 1.4426950408889634

F32 = jnp.float32
BF16 = jnp.bfloat16


def _rms(x, eps):
    return x * lax.rsqrt(jnp.mean(x * x, axis=-1, keepdims=True) + eps)


def _resident(shape):
    return pl.BlockSpec(shape, lambda *_: (0,) * len(shape), pipeline_mode=pl.Buffered(1))


def _ada_kernel(c_ref, w_ref, b_ref, o_ref):
    c = c_ref[...]
    ca = c * jax.nn.sigmoid(c)
    batch = ca.shape[0]
    ca_hi = ca.astype(BF16)
    ca_lo = (ca - ca_hi.astype(F32)).astype(BF16)
    w = w_ref[0]
    w_hi = w.astype(BF16)
    w_lo = (w - w_hi.astype(F32)).astype(BF16)
    p = jnp.dot(jnp.concatenate([ca_hi, ca_lo], axis=0), w_hi, preferred_element_type=F32)
    q = jnp.dot(ca_hi, w_lo, preferred_element_type=F32)
    o_ref[0] = p[:batch] + p[batch:] + q + b_ref[0]


def _ada_call(c, w_ada, b_ada):
    depth, _, n_out = w_ada.shape
    batch = c.shape[0]
    return pl.pallas_call(
        _ada_kernel,
        grid=(depth, n_out // ADA_COLS),
        in_specs=[
            pl.BlockSpec((batch, D_MODEL), lambda l, j: (0, 0)),
            pl.BlockSpec((1, D_MODEL, ADA_COLS), lambda l, j: (l, 0, j)),
            pl.BlockSpec((1, 1, ADA_COLS), lambda l, j: (l, 0, j)),
        ],
        out_specs=pl.BlockSpec((1, batch, ADA_COLS), lambda l, j: (l, 0, j)),
        out_shape=jax.ShapeDtypeStruct((depth, batch, n_out), F32),
        compiler_params=pltpu.CompilerParams(
            dimension_semantics=("arbitrary", "arbitrary"), vmem_limit_bytes=VMEM_LIMIT),
        name="ada",
    )(c, w_ada, b_ada.reshape(depth, 1, n_out))


def _stream_cast(n_blocks, src_block, dst_ref, stage_ref, sem_ref):
    def copy(i, slot):
        return pltpu.make_async_copy(src_block(i), stage_ref.at[slot], sem_ref.at[slot])

    copy(0, 0).start()

    def body(i, carry):
        slot = i % 2

        @pl.when(i + 1 < n_blocks)
        def _():
            copy(i + 1, 1 - slot).start()

        copy(i, slot).wait()
        dst_ref[i] = stage_ref[slot].astype(BF16)
        return carry

    lax.fori_loop(0, n_blocks, body, 0)


def _col_block(w_hbm, layer, width):
    return lambda i: w_hbm.at[layer, :, pl.ds(pl.multiple_of(i * width, width), width)]


def _row_block(w_hbm, layer, height):
    return lambda i: w_hbm.at[layer, pl.ds(pl.multiple_of(i * height, height), height), :]


def _cols(w_ref, c0, width):
    block_width = w_ref.shape[2]
    blk, off = divmod(c0, block_width)
    assert off + width <= block_width
    return w_ref[blk, :, off:off + width]


def _ffn_kernel(h_ref, mod_ref, ng_ref, w13_hbm, w2_hbm, fg_ref, o_ref, w13_ref, w2_ref,
                stage13_ref, stage2_ref, sem13_ref, sem2_ref, *, layer, mod_base, final):
    @pl.when(pl.program_id(0) == 0)
    def _():
        _stream_cast(w13_ref.shape[0], _col_block(w13_hbm, layer, W13_BLOCK), w13_ref,
                     stage13_ref, sem13_ref)
        _stream_cast(w2_ref.shape[0], _row_block(w2_hbm, layer, FF_CHUNK), w2_ref,
                     stage2_ref, sem2_ref)

    h = h_ref[...]
    sh = mod_ref[0, mod_base:mod_base + 1, :]
    sc = mod_ref[0, mod_base + 1:mod_base + 2, :]
    gate = mod_ref[0, mod_base + 2:mod_base + 3, :]
    y = _rms(h, RMS_EPS) * ng_ref[...]
    yb = (y * (1.0 + sc) + sh).astype(BF16)
    acc = None
    for j in range(N_FF_CHUNKS):
        c0 = j * FF_CHUNK
        a = jnp.dot(yb, _cols(w13_ref, c0, FF_CHUNK), preferred_element_type=F32)
        b = jnp.dot(yb, _cols(w13_ref, D_FF + c0, FF_CHUNK), preferred_element_type=F32)
        hm = (a * jax.nn.sigmoid(a) * b).astype(BF16)
        d = jnp.dot(hm, w2_ref[j], preferred_element_type=F32)
        acc = d if acc is None else acc + d
    res = h_ref[...] + 0.5 * gate * acc
    if final:
        res = _rms(res, RMS_EPS) * fg_ref[...]
    o_ref[...] = res


def _ffn_call(h2d, mod, ngain, w13, w2, fgain, *, layer, mod_base, final):
    n_tok = h2d.shape[0]
    tiles_per_seq = SEQ // FFN_ROWS
    kern = functools.partial(_ffn_kernel, layer=layer, mod_base=mod_base, final=final)
    return pl.pallas_call(
        kern,
        grid=(n_tok // FFN_ROWS,),
        in_specs=[
            pl.BlockSpec((FFN_ROWS, D_MODEL), lambda i: (i, 0)),
            pl.BlockSpec((1, N_MOD, D_MODEL), lambda i: (i // tiles_per_seq, 0, 0)),
            _resident((1, D_MODEL)),
            pl.BlockSpec(memory_space=pl.ANY),
            pl.BlockSpec(memory_space=pl.ANY),
            _resident((1, D_MODEL)),
        ],
        out_specs=pl.BlockSpec((FFN_ROWS, D_MODEL), lambda i: (i, 0)),
        out_shape=jax.ShapeDtypeStruct(h2d.shape, F32),
        scratch_shapes=[
            pltpu.VMEM((2 * D_FF // W13_BLOCK, D_MODEL, W13_BLOCK), BF16),
            pltpu.VMEM((N_FF_CHUNKS, FF_CHUNK, D_MODEL), BF16),
            pltpu.VMEM((2, D_MODEL, W13_BLOCK), F32),
            pltpu.VMEM((2, FF_CHUNK, D_MODEL), F32),
            pltpu.SemaphoreType.DMA((2,)),
            pltpu.SemaphoreType.DMA((2,)),
        ],
        compiler_params=pltpu.CompilerParams(
            dimension_semantics=("arbitrary",), vmem_limit_bytes=VMEM_LIMIT),
        name="ffn_final" if final else "ffn",
    )(h2d, mod, ngain, w13, w2, fgain)


def _pair_masks():
    r = np.arange(CHUNK)
    pair = np.zeros((len(MATMUL_SIZES), CHUNK, CHUNK), np.float32)
    for i, m in enumerate(MATMUL_SIZES):
        blk = r // (2 * m)
        low = (r % (2 * m)) < m
        pair[i] = (blk[:, None] == blk[None, :]) & (~low)[:, None] & low[None, :]
    near = np.zeros((NEAR_ROWS, CHUNK, CHUNK), np.float32)
    for d in range(NEAR_ROWS):
        near[d] = (r[:, None] - r[None, :] == d) & ((r % NEAR_ROWS) >= d)[:, None]
    return np.concatenate([pair, near], axis=0)


def _earlier_rows(x, d):
    grouped = x.reshape(x.shape[0] // SUBLANES, SUBLANES, x.shape[1])
    return pltpu.roll(grouped, d, 1).reshape(x.shape)


def _prefix_rows(x):
    rows = x.shape[0]
    row = lax.broadcasted_iota(jnp.int32, x.shape, 0) % SUBLANES
    d = 1
    while d < SUBLANES:
        x = x + jnp.where(row >= d, _earlier_rows(x, d), 0.0)
        d *= 2
    parts = [x[0:SUBLANES]]
    for g in range(SUBLANES, rows, SUBLANES):
        parts.append(x[g:g + SUBLANES] + parts[-1][SUBLANES - 1:SUBLANES, :])
    return jnp.concatenate(parts, axis=0)


def _level_log_decay(b, m):
    rows, width = b.shape
    if m >= SUBLANES:
        parts = []
        for r0 in range(0, rows, 2 * m):
            bref = b[r0 + m - 1:r0 + m, :]
            parts.append(bref - b[r0:r0 + m])
            parts.append(b[r0 + m:r0 + 2 * m] - bref)
        return jnp.concatenate(parts, axis=0)
    row = lax.broadcasted_iota(jnp.int32, b.shape, 0)

    def group_row(i):
        return jnp.concatenate(
            [jnp.broadcast_to(b[g + i:g + i + 1, :], (SUBLANES, width))
             for g in range(0, rows, SUBLANES)], axis=0)

    bref = group_row(m - 1)
    for blk in range(1, SUBLANES // (2 * m)):
        bref = jnp.where((row % SUBLANES) >= blk * 2 * m, group_row(blk * 2 * m + m - 1), bref)
    return jnp.where((row & m) != 0, b - bref, bref - b)


def _select_rows(m, upper_val, lower_val):
    rows = upper_val.shape[0]
    if m >= SUBLANES:
        parts = []
        for r0 in range(0, rows, m):
            src = upper_val if (r0 // m) % 2 == 1 else lower_val
            parts.append(src[r0:r0 + m])
        return jnp.concatenate(parts, axis=0)
    row = lax.broadcasted_iota(jnp.int32, upper_val.shape, 0)
    return jnp.where((row & m) != 0, upper_val, lower_val)


def _mixer_kernel(h_ref, mod_ref, ng_ref, win_hbm, wout_hbm, lbl_ref, gn_ref, lng_ref, og_ref,
                  wsp_ref, bsp_ref, pair_ref, o_ref, st_ref, proj_ref, win_ref, wout_ref,
                  stage_in_ref, stage_out_ref, sem_in_ref, sem_out_ref, *, layer):
    @pl.when((pl.program_id(0) == 0) & (pl.program_id(1) == 0))
    def _():
        _stream_cast(win_ref.shape[0], _col_block(win_hbm, layer, WIN_BLOCK), win_ref,
                     stage_in_ref, sem_in_ref)
        _stream_cast(wout_ref.shape[0], _col_block(wout_hbm, layer, WOUT_BLOCK), wout_ref,
                     stage_out_ref, sem_out_ref)

    @pl.when(pl.program_id(1) == 0)
    def _():
        st_ref[...] = jnp.zeros_like(st_ref)

    sh = mod_ref[0, 3:4, :]
    sc = mod_ref[0, 4:5, :]
    gate = mod_ref[0, 5:6, :]

    def normalize(r0):
        y = _rms(h_ref[0, r0:r0 + SUB_ROWS, :], RMS_EPS) * ng_ref[...]
        return (y * (1.0 + sc) + sh).astype(BF16)

    def proj(r0, rows, c0, c1):
        return proj_ref[r0:r0 + rows, c0:c1]

    def project(r0, y, c0, c1):
        for c in range(c0, c1, WIN_BLOCK):
            proj_ref[r0:r0 + SUB_ROWS, c:c + WIN_BLOCK] = jnp.dot(
                y, _cols(win_ref, c, WIN_BLOCK), preferred_element_type=F32)

    def out_project(r0, cat, c0, c1):
        mix = jnp.dot(cat, _cols(wout_ref, c0, c1 - c0), preferred_element_type=F32)
        o_ref[0, r0:r0 + SUB_ROWS, c0:c1] = h_ref[0, r0:r0 + SUB_ROWS, c0:c1] + gate[:, c0:c1] * mix

    lg = lbl_ref[...]
    ex = jnp.exp(lg - jnp.max(lg, axis=0, keepdims=True))
    prob = ex / jnp.sum(ex, axis=0, keepdims=True)
    lb = jnp.zeros((1, HG_WIDTH), F32)
    for i in range(1, layer + 1):
        lb = lb + prob[i:i + 1, :]

    r2 = lax.broadcasted_iota(jnp.int32, (GM_BLOCK, GM_BLOCK), 0)
    c2 = lax.broadcasted_iota(jnp.int32, (GM_BLOCK, GM_BLOCK), 1)
    causal = (r2 // CHUNK) >= (c2 // CHUNK)
    w_sp = [jnp.where(causal, wsp_ref[hd], 0.0).astype(BF16) for hd in range(GM_HEADS)]

    def gm_mix(r0, n_blocks, head_ids):
        res = [[] for _ in range(n_blocks)]
        for hd in head_ids:
            cu = 4 * HG_WIDTH + hd * GM_DH
            cv = cu + GM_WIDTH
            us, vns = [], []
            for blk in range(n_blocks):
                rb = r0 + blk * GM_BLOCK
                us.append(jax.nn.gelu(proj(rb, GM_BLOCK, cu, cu + GM_DH)))
                v = jax.nn.gelu(proj(rb, GM_BLOCK, cv, cv + GM_DH))
                vc = v - jnp.mean(v, axis=-1, keepdims=True)
                var = jnp.mean(vc * vc, axis=-1, keepdims=True)
                vn = vc * lax.rsqrt(var + LN_EPS) * lng_ref[:, hd * GM_DH:(hd + 1) * GM_DH]
                vns.append(vn.astype(BF16))
            mixed = jnp.dot(w_sp[hd], jnp.concatenate(vns, axis=1), preferred_element_type=F32)
            for blk in range(n_blocks):
                res[blk].append((us[blk], mixed[:, blk * GM_DH:(blk + 1) * GM_DH] + bsp_ref[hd]))
        return res

    def gm_gate(res):
        outs = []
        for hd, (u, mixed) in enumerate(res):
            z = _rms(u * mixed, RMS_EPS) * og_ref[:, hd * GM_DH:(hd + 1) * GM_DH]
            outs.append(z.astype(BF16))
        return jnp.concatenate(outs, axis=1)

    def hg_gates(r0):
        q = proj(r0, CHUNK, 0, HG_WIDTH)
        f_raw = proj(r0, CHUNK, HG_WIDTH, 2 * HG_WIDTH)
        f = lb + (1.0 - lb) * jax.nn.sigmoid(f_raw)
        log_f = jnp.log(f) * LOG2_E
        k = 1.0 - f
        return r0, q, k, f, _prefix_rows(log_f)

    def hg_scores(gates):
        r0, q_all, k_all, f_all, b_all = gates
        dn = (((1,), (1,)), ((), ()))
        per_head = []
        for hd in range(HG_HEADS):
            cs = slice(hd * HG_DK, (hd + 1) * HG_DK)
            q, k, f, b = q_all[:, cs], k_all[:, cs], f_all[:, cs], b_all[:, cs]
            decay_in = jnp.exp2(b)
            decay_out = jnp.exp2(b[CHUNK - 1:CHUNK, :] - b)
            att = jnp.sum(q * k, axis=-1, keepdims=True) * pair_ref[len(MATMUL_SIZES)]
            decay = None
            for d in range(1, NEAR_ROWS):
                decay = f if d == 1 else decay * _earlier_rows(f, d - 1)
                att = att + (jnp.sum(q * _earlier_rows(k, d) * decay, axis=-1, keepdims=True)
                             * pair_ref[len(MATMUL_SIZES) + d])
            for i, m in enumerate(MATMUL_SIZES):
                mh = (_select_rows(m, q, k) * jnp.exp2(_level_log_decay(b, m))).astype(BF16)
                att = att + lax.dot_general(mh, mh, dn, preferred_element_type=F32) * pair_ref[i]
            vb = proj(r0, CHUNK, 2 * HG_WIDTH + hd * HG_DK, 2 * HG_WIDTH + (hd + 1) * HG_DK)
            per_head.append((att.astype(BF16), vb.astype(BF16), (q * decay_in).astype(BF16),
                             (k * decay_out).astype(BF16), decay_in[CHUNK - 1:CHUNK, :]))
        return r0, per_head

    def hg_output(scored):
        r0, per_head = scored
        dn = (((1,), (1,)), ((), ()))
        outs = []
        for hd, (att, vb, q_in, k_out, decay_last) in enumerate(per_head):
            cs = slice(hd * HG_DK, (hd + 1) * HG_DK)
            st = st_ref[hd]
            o = (jnp.dot(att, vb, preferred_element_type=F32)
                 + lax.dot_general(q_in, st.astype(BF16), dn, preferred_element_type=F32))
            inc = lax.dot_general(vb, k_out, (((0,), (0,)), ((), ())),
                                  preferred_element_type=F32)
            st_ref[hd] = st * decay_last + inc
            half_g = 0.5 * proj(r0, CHUNK, 3 * HG_WIDTH + hd * HG_DK, 3 * HG_WIDTH + (hd + 1) * HG_DK)
            g_act = half_g + half_g * jnp.tanh(half_g)
            o = _rms(o, RMS_EPS) * gn_ref[:, cs] * g_act
            outs.append(o.astype(BF16))
        return jnp.concatenate(outs, axis=1)

    n_sub = MIX_ROWS // SUB_ROWS
    n_chunks = MIX_ROWS // CHUNK
    chunks_per_sub = SUB_ROWS // CHUNK
    chunks_per_blk = GM_BLOCK // CHUNK
    blocks_per_sub = SUB_ROWS // GM_BLOCK
    in_piece = IN_COLS // chunks_per_sub
    out_piece = D_MODEL // chunks_per_sub
    lag_out = 2

    y = [normalize(s * SUB_ROWS) for s in range(n_sub)]
    project(0, y[0], 0, IN_COLS)
    gates, scores, gm_state, hg_out, gm_out, cat = {}, {}, {}, {}, {}, {}

    def big_projections(slot):
        s_next = slot // chunks_per_sub + 1
        if s_next < n_sub:
            p = slot % chunks_per_sub
            project(s_next * SUB_ROWS, y[s_next], p * in_piece, (p + 1) * in_piece)
        done = slot - lag_out - chunks_per_sub
        if done >= 0 and done // chunks_per_sub < n_sub:
            s_done, p = divmod(done, chunks_per_sub)
            if p == 0:
                c0 = s_done * chunks_per_sub
                hg_part = jnp.concatenate([hg_out.pop(c0 + i) for i in range(chunks_per_sub)], axis=0)
                gm_part = jnp.concatenate(
                    [gm_out.pop(c0 + i) for i in range(0, chunks_per_sub, chunks_per_blk)], axis=0)
                cat[s_done] = jnp.concatenate([hg_part, gm_part], axis=1)
            out_project(s_done * SUB_ROWS, cat[s_done], p * out_piece, (p + 1) * out_piece)

    for slot in range(n_chunks + lag_out + chunks_per_sub):
        big_projections(slot)
        if 0 <= slot - lag_out < n_chunks:
            hg_out[slot - lag_out] = hg_output(scores.pop(slot - lag_out))
        if 0 <= slot - 1 < n_chunks:
            scores[slot - 1] = hg_scores(gates.pop(slot - 1))
        if slot < n_chunks:
            gates[slot] = hg_gates(slot * CHUNK)
            p = slot % chunks_per_sub
            if p < 2:
                head_ids = range(p * GM_HEADS // 2, (p + 1) * GM_HEADS // 2)
                for i, res in enumerate(gm_mix((slot - p) * CHUNK, blocks_per_sub, head_ids)):
                    gm_state.setdefault(slot - p + i * chunks_per_blk, []).extend(res)
            else:
                key = slot - p + (p - 2) * chunks_per_blk
                gm_out[key] = gm_gate(gm_state.pop(key))


def _mixer_call(h, mod, ngain, w_in, w_out, lb_logits, gnorm, ln_gain, out_gain, w_sp, b_sp,
                pair_c, *, layer):
    batch = h.shape[0]
    depth = lb_logits.shape[0]
    kern = functools.partial(_mixer_kernel, layer=layer)
    return pl.pallas_call(
        kern,
        grid=(batch, SEQ // MIX_ROWS),
        in_specs=[
            pl.BlockSpec((1, MIX_ROWS, D_MODEL), lambda b, j: (b, j, 0)),
            pl.BlockSpec((1, N_MOD, D_MODEL), lambda b, j: (b, 0, 0)),
            _resident((1, D_MODEL)),
            pl.BlockSpec(memory_space=pl.ANY),
            pl.BlockSpec(memory_space=pl.ANY),
            _resident((depth, HG_WIDTH)),
            _resident((1, HG_WIDTH)),
            _resident((1, GM_WIDTH)),
            _resident((1, GM_WIDTH)),
            _resident((GM_HEADS, GM_BLOCK, GM_BLOCK)),
            _resident((GM_HEADS, GM_BLOCK, 1)),
            _resident(pair_c.shape),
        ],
        out_specs=pl.BlockSpec((1, MIX_ROWS, D_MODEL), lambda b, j: (b, j, 0)),
        out_shape=jax.ShapeDtypeStruct(h.shape, F32),
        scratch_shapes=[pltpu.VMEM((HG_HEADS, HG_DK, HG_DK), F32),
                        pltpu.VMEM((MIX_ROWS, IN_COLS), F32),
                        pltpu.VMEM((IN_COLS // WIN_BLOCK, D_MODEL, WIN_BLOCK), BF16),
                        pltpu.VMEM((D_MODEL // WOUT_BLOCK, D_MODEL, WOUT_BLOCK), BF16),
                        pltpu.VMEM((2, D_MODEL, WIN_BLOCK), F32),
                        pltpu.VMEM((2, D_MODEL, WOUT_BLOCK), F32),
                        pltpu.SemaphoreType.DMA((2,)),
                        pltpu.SemaphoreType.DMA((2,))],
        compiler_params=pltpu.CompilerParams(
            dimension_semantics=("arbitrary", "arbitrary"), vmem_limit_bytes=VMEM_LIMIT),
        name="mixer",
    )(h, mod, ngain, w_in, w_out, lb_logits, gnorm, ln_gain, out_gain, w_sp, b_sp, pair_c)


def kernel(x, c, w_ada, b_ada, norm_gain, ffn1_w13, ffn1_w2, w_in, hg_lb_logits, hg_gnorm,
           gm_ln_gain, gm_w_spatial, gm_b_spatial, gm_out_gain, w_out, ffn2_w13, ffn2_w2,
           final_gain):
    batch, seq, _ = x.shape
    depth = w_ada.shape[0]
    pair_c = jnp.asarray(_pair_masks(), F32)
    fgain = final_gain.reshape(1, D_MODEL)

    mod_all = _ada_call(c, w_ada, b_ada).reshape(depth, batch, N_MOD, D_MODEL)
    h = x
    for l in range(depth):
        mod = mod_all[l]
        h = _ffn_call(h.reshape(batch * seq, D_MODEL), mod, norm_gain[l, 0:1], ffn1_w13, ffn1_w2,
                      fgain, layer=l, mod_base=0, final=False).reshape(batch, seq, D_MODEL)
        h = _mixer_call(h, mod, norm_gain[l, 1:2], w_in, w_out,
                        hg_lb_logits, hg_gnorm[l:l + 1], gm_ln_gain[l:l + 1], gm_out_gain[l:l + 1],
                        gm_w_spatial[l], gm_b_spatial[l].reshape(GM_HEADS, GM_BLOCK, 1),
                        pair_c, layer=l)
        h = _ffn_call(h.reshape(batch * seq, D_MODEL), mod, norm_gain[l, 2:3], ffn2_w13, ffn2_w2,
                      fgain, layer=l, mod_base=6,
                      final=(l == depth - 1)).reshape(batch, seq, D_MODEL)
    return h
```

```python
import functools

import numpy as np
import jax
import jax.numpy as jnp
from jax import lax
from jax.experimental import pallas as pl
from jax.experimental.pallas import tpu as pltpu

D_MODEL = 1024
SEQ = 2048
CHUNK = 64
HG_WIDTH = 512
HG_HEADS = 4
HG_DK = 128
GM_WIDTH = 512
GM_HEADS = 4
GM_DH = 128
GM_BLOCK = 128
D_FF = 2816
N_MOD = 9
IN_COLS = 4 * HG_WIDTH + 2 * GM_WIDTH
RMS_EPS = 1e-6
LN_EPS = 1e-5

FF_CHUNK = 256
N_FF_CHUNKS = D_FF // FF_CHUNK
W13_BLOCK = 2 * FF_CHUNK
WIN_BLOCK = 768
WOUT_BLOCK = 256
FFN_ROWS = 1024
MIX_ROWS = 512
SUB_ROWS = 256
ADA_COLS = 2304
NEAR_ROWS = 4
MATMUL_SIZES = (32, 16, 8, 4)
SUBLANES = 8
VMEM_LIMIT = 56 * 1024 * 1024
LOG2_E = 1.4426950408889634

F32 = jnp.float32
BF16 = jnp.bfloat16


def _rms(x, eps):
    return x * lax.rsqrt(jnp.mean(x * x, axis=-1, keepdims=True) + eps)


def _resident(shape):
    return pl.BlockSpec(shape, lambda *_: (0,) * len(shape), pipeline_mode=pl.Buffered(1))


def _ada_kernel(c_ref, w_ref, b_ref, o_ref):
    c = c_ref[...]
    ca = c * jax.nn.sigmoid(c)
    batch = ca.shape[0]
    ca_hi = ca.astype(BF16)
    ca_lo = (ca - ca_hi.astype(F32)).astype(BF16)
    w = w_ref[0]
    w_hi = w.astype(BF16)
    w_lo = (w - w_hi.astype(F32)).astype(BF16)
    p = jnp.dot(jnp.concatenate([ca_hi, ca_lo], axis=0), w_hi, preferred_element_type=F32)
    q = jnp.dot(ca_hi, w_lo, preferred_element_type=F32)
    o_ref[0] = p[:batch] + p[batch:] + q + b_ref[0]


def _ada_call(c, w_ada, b_ada):
    depth, _, n_out = w_ada.shape
    batch = c.shape[0]
    return pl.pallas_call(
        _ada_kernel,
        grid=(depth, n_out // ADA_COLS),
        in_specs=[
            pl.BlockSpec((batch, D_MODEL), lambda l, j: (0, 0)),
            pl.BlockSpec((1, D_MODEL, ADA_COLS), lambda l, j: (l, 0, j)),
            pl.BlockSpec((1, 1, ADA_COLS), lambda l, j: (l, 0, j)),
        ],
        out_specs=pl.BlockSpec((1, batch, ADA_COLS), lambda l, j: (l, 0, j)),
        out_shape=jax.ShapeDtypeStruct((depth, batch, n_out), F32),
        compiler_params=pltpu.CompilerParams(
            dimension_semantics=("arbitrary", "arbitrary"), vmem_limit_bytes=VMEM_LIMIT),
        name="ada",
    )(c, w_ada, b_ada.reshape(depth, 1, n_out))


def _stream_cast(n_blocks, src_block, dst_ref, stage_ref, sem_ref):
    def copy(i, slot):
        return pltpu.make_async_copy(src_block(i), stage_ref.at[slot], sem_ref.at[slot])

    copy(0, 0).start()

    def body(i, carry):
        slot = i % 2

        @pl.when(i + 1 < n_blocks)
        def _():
            copy(i + 1, 1 - slot).start()

        copy(i, slot).wait()
        dst_ref[i] = stage_ref[slot].astype(BF16)
        return carry

    lax.fori_loop(0, n_blocks, body, 0)


def _col_block(w_hbm, layer, width):
    return lambda i: w_hbm.at[layer, :, pl.ds(pl.multiple_of(i * width, width), width)]


def _row_block(w_hbm, layer, height):
    return lambda i: w_hbm.at[layer, pl.ds(pl.multiple_of(i * height, height), height), :]


def _cols(w_ref, c0, width):
    block_width = w_ref.shape[2]
    blk, off = divmod(c0, block_width)
    assert off + width <= block_width
    return w_ref[blk, :, off:off + width]


def _ffn_kernel(h_ref, mod_ref, ng_ref, w13_hbm, w2_hbm, fg_ref, o_ref, w13_ref, w2_ref,
                stage13_ref, stage2_ref, sem13_ref, sem2_ref, *, layer, mod_base, final):
    @pl.when(pl.program_id(0) == 0)
    def _():
        _stream_cast(w13_ref.shape[0], _col_block(w13_hbm, layer, W13_BLOCK), w13_ref,
                     stage13_ref, sem13_ref)
        _stream_cast(w2_ref.shape[0], _row_block(w2_hbm, layer, FF_CHUNK), w2_ref,
                     stage2_ref, sem2_ref)

    h = h_ref[...]
    sh = mod_ref[0, mod_base:mod_base + 1, :]
    sc = mod_ref[0, mod_base + 1:mod_base + 2, :]
    gate = mod_ref[0, mod_base + 2:mod_base + 3, :]
    y = _rms(h, RMS_EPS) * ng_ref[...]
    yb = (y * (1.0 + sc) + sh).astype(BF16)
    acc = None
    for j in range(N_FF_CHUNKS):
        c0 = j * FF_CHUNK
        a = jnp.dot(yb, _cols(w13_ref, c0, FF_CHUNK), preferred_element_type=F32)
        b = jnp.dot(yb, _cols(w13_ref, D_FF + c0, FF_CHUNK), preferred_element_type=F32)
        hm = (a * jax.nn.sigmoid(a) * b).astype(BF16)
        d = jnp.dot(hm, w2_ref[j], preferred_element_type=F32)
        acc = d if acc is None else acc + d
    res = h_ref[...] + 0.5 * gate * acc
    if final:
        res = _rms(res, RMS_EPS) * fg_ref[...]
    o_ref[...] = res


def _ffn_call(h2d, mod, ngain, w13, w2, fgain, *, layer, mod_base, final):
    n_tok = h2d.shape[0]
    tiles_per_seq = SEQ // FFN_ROWS
    kern = functools.partial(_ffn_kernel, layer=layer, mod_base=mod_base, final=final)
    return pl.pallas_call(
        kern,
        grid=(n_tok // FFN_ROWS,),
        in_specs=[
            pl.BlockSpec((FFN_ROWS, D_MODEL), lambda i: (i, 0)),
            pl.BlockSpec((1, N_MOD, D_MODEL), lambda i: (i // tiles_per_seq, 0, 0)),
            _resident((1, D_MODEL)),
            pl.BlockSpec(memory_space=pl.ANY),
            pl.BlockSpec(memory_space=pl.ANY),
            _resident((1, D_MODEL)),
        ],
        out_specs=pl.BlockSpec((FFN_ROWS, D_MODEL), lambda i: (i, 0)),
        out_shape=jax.ShapeDtypeStruct(h2d.shape, F32),
        scratch_shapes=[
            pltpu.VMEM((2 * D_FF // W13_BLOCK, D_MODEL, W13_BLOCK), BF16),
            pltpu.VMEM((N_FF_CHUNKS, FF_CHUNK, D_MODEL), BF16),
            pltpu.VMEM((2, D_MODEL, W13_BLOCK), F32),
            pltpu.VMEM((2, FF_CHUNK, D_MODEL), F32),
            pltpu.SemaphoreType.DMA((2,)),
            pltpu.SemaphoreType.DMA((2,)),
        ],
        compiler_params=pltpu.CompilerParams(
            dimension_semantics=("arbitrary",), vmem_limit_bytes=VMEM_LIMIT),
        name="ffn_final" if final else "ffn",
    )(h2d, mod, ngain, w13, w2, fgain)


def _pair_masks():
    r = np.arange(CHUNK)
    pair = np.zeros((len(MATMUL_SIZES), CHUNK, CHUNK), np.float32)
    for i, m in enumerate(MATMUL_SIZES):
        blk = r // (2 * m)
        low = (r % (2 * m)) < m
        pair[i] = (blk[:, None] == blk[None, :]) & (~low)[:, None] & low[None, :]
    near = np.zeros((NEAR_ROWS, CHUNK, CHUNK), np.float32)
    for d in range(NEAR_ROWS):
        near[d] = (r[:, None] - r[None, :] == d) & ((r % NEAR_ROWS) >= d)[:, None]
    return np.concatenate([pair, near], axis=0)


def _earlier_rows(x, d):
    grouped = x.reshape(x.shape[0] // SUBLANES, SUBLANES, x.shape[1])
    return pltpu.roll(grouped, d, 1).reshape(x.shape)


def _prefix_rows(x):
    rows = x.shape[0]
    row = lax.broadcasted_iota(jnp.int32, x.shape, 0) % SUBLANES
    d = 1
    while d < SUBLANES:
        x = x + jnp.where(row >= d, _earlier_rows(x, d), 0.0)
        d *= 2
    parts = [x[0:SUBLANES]]
    for g in range(SUBLANES, rows, SUBLANES):
        parts.append(x[g:g + SUBLANES] + parts[-1][SUBLANES - 1:SUBLANES, :])
    return jnp.concatenate(parts, axis=0)


def _level_log_decay(b, m):
    rows, width = b.shape
    if m >= SUBLANES:
        parts = []
        for r0 in range(0, rows, 2 * m):
            bref = b[r0 + m - 1:r0 + m, :]
            parts.append(bref - b[r0:r0 + m])
            parts.append(b[r0 + m:r0 + 2 * m] - bref)
        return jnp.concatenate(parts, axis=0)
    row = lax.broadcasted_iota(jnp.int32, b.shape, 0)

    def group_row(i):
        return jnp.concatenate(
            [jnp.broadcast_to(b[g + i:g + i + 1, :], (SUBLANES, width))
             for g in range(0, rows, SUBLANES)], axis=0)

    bref = group_row(m - 1)
    for blk in range(1, SUBLANES // (2 * m)):
        bref = jnp.where((row % SUBLANES) >= blk * 2 * m, group_row(blk * 2 * m + m - 1), bref)
    return jnp.where((row & m) != 0, b - bref, bref - b)


def _select_rows(m, upper_val, lower_val):
    rows = upper_val.shape[0]
    if m >= SUBLANES:
        parts = []
        for r0 in range(0, rows, m):
            src = upper_val if (r0 // m) % 2 == 1 else lower_val
            parts.append(src[r0:r0 + m])
        return jnp.concatenate(parts, axis=0)
    row = lax.broadcasted_iota(jnp.int32, upper_val.shape, 0)
    return jnp.where((row & m) != 0, upper_val, lower_val)


def _mixer_kernel(h_ref, mod_ref, ng_ref, win_hbm, wout_hbm, lbl_ref, gn_ref, lng_ref, og_ref,
                  wsp_ref, bsp_ref, pair_ref, o_ref, st_ref, proj_ref, win_ref, wout_ref,
                  stage_in_ref, stage_out_ref, sem_in_ref, sem_out_ref, *, layer):
    @pl.when((pl.program_id(0) == 0) & (pl.program_id(1) == 0))
    def _():
        _stream_cast(win_ref.shape[0], _col_block(win_hbm, layer, WIN_BLOCK), win_ref,
                     stage_in_ref, sem_in_ref)
        _stream_cast(wout_ref.shape[0], _col_block(wout_hbm, layer, WOUT_BLOCK), wout_ref,
                     stage_out_ref, sem_out_ref)

    @pl.when(pl.program_id(1) == 0)
    def _():
        st_ref[...] = jnp.zeros_like(st_ref)

    sh = mod_ref[0, 3:4, :]
    sc = mod_ref[0, 4:5, :]
    gate = mod_ref[0, 5:6, :]

    def normalize(r0):
        y = _rms(h_ref[0, r0:r0 + SUB_ROWS, :], RMS_EPS) * ng_ref[...]
        return (y * (1.0 + sc) + sh).astype(BF16)

    def proj(r0, rows, c0, c1):
        return proj_ref[r0:r0 + rows, c0:c1]

    def project(r0, y, c0, c1):
        for c in range(c0, c1, WIN_BLOCK):
            proj_ref[r0:r0 + SUB_ROWS, c:c + WIN_BLOCK] = jnp.dot(
                y, _cols(win_ref, c, WIN_BLOCK), preferred_element_type=F32)

    def out_project(r0, cat, c0, c1):
        mix = jnp.dot(cat, _cols(wout_ref, c0, c1 - c0), preferred_element_type=F32)
        o_ref[0, r0:r0 + SUB_ROWS, c0:c1] = h_ref[0, r0:r0 + SUB_ROWS, c0:c1] + gate[:, c0:c1] * mix

    lg = lbl_ref[...]
    ex = jnp.exp(lg - jnp.max(lg, axis=0, keepdims=True))
    prob = ex / jnp.sum(ex, axis=0, keepdims=True)
    lb = jnp.zeros((1, HG_WIDTH), F32)
    for i in range(1, layer + 1):
        lb = lb + prob[i:i + 1, :]

    r2 = lax.broadcasted_iota(jnp.int32, (GM_BLOCK, GM_BLOCK), 0)
    c2 = lax.broadcasted_iota(jnp.int32, (GM_BLOCK, GM_BLOCK), 1)
    causal = (r2 // CHUNK) >= (c2 // CHUNK)
    w_sp = [jnp.where(causal, wsp_ref[hd], 0.0).astype(BF16) for hd in range(GM_HEADS)]

    def gm_mix(r0, n_blocks, head_ids):
        res = [[] for _ in range(n_blocks)]
        for hd in head_ids:
            cu = 4 * HG_WIDTH + hd * GM_DH
            cv = cu + GM_WIDTH
            us, vns = [], []
            for blk in range(n_blocks):
                rb = r0 + blk * GM_BLOCK
                us.append(jax.nn.gelu(proj(rb, GM_BLOCK, cu, cu + GM_DH)))
                v = jax.nn.gelu(proj(rb, GM_BLOCK, cv, cv + GM_DH))
                vc = v - jnp.mean(v, axis=-1, keepdims=True)
                var = jnp.mean(vc * vc, axis=-1, keepdims=True)
                vn = vc * lax.rsqrt(var + LN_EPS) * lng_ref[:, hd * GM_DH:(hd + 1) * GM_DH]
                vns.append(vn.astype(BF16))
            mixed = jnp.dot(w_sp[hd], jnp.concatenate(vns, axis=1), preferred_element_type=F32)
            for blk in range(n_blocks):
                res[blk].append((us[blk], mixed[:, blk * GM_DH:(blk + 1) * GM_DH] + bsp_ref[hd]))
        return res

    def gm_gate(res):
        outs = []
        for hd, (u, mixed) in enumerate(res):
            z = _rms(u * mixed, RMS_EPS) * og_ref[:, hd * GM_DH:(hd + 1) * GM_DH]
            outs.append(z.astype(BF16))
        return jnp.concatenate(outs, axis=1)

    def hg_gates(r0):
        q = proj(r0, CHUNK, 0, HG_WIDTH)
        f_raw = proj(r0, CHUNK, HG_WIDTH, 2 * HG_WIDTH)
        f = lb + (1.0 - lb) * jax.nn.sigmoid(f_raw)
        log_f = jnp.log(f) * LOG2_E
        k = 1.0 - f
        return r0, q, k, f, _prefix_rows(log_f)

    def hg_scores(gates):
        r0, q_all, k_all, f_all, b_all = gates
        dn = (((1,), (1,)), ((), ()))
        per_head = []
        for hd in range(HG_HEADS):
            cs = slice(hd * HG_DK, (hd + 1) * HG_DK)
            q, k, f, b = q_all[:, cs], k_all[:, cs], f_all[:, cs], b_all[:, cs]
            decay_in = jnp.exp2(b)
            decay_out = jnp.exp2(b[CHUNK - 1:CHUNK, :] - b)
            att = jnp.sum(q * k, axis=-1, keepdims=True) * pair_ref[len(MATMUL_SIZES)]
            decay = None
            for d in range(1, NEAR_ROWS):
                decay = f if d == 1 else decay * _earlier_rows(f, d - 1)
                att = att + (jnp.sum(q * _earlier_rows(k, d) * decay, axis=-1, keepdims=True)
                             * pair_ref[len(MATMUL_SIZES) + d])
            for i, m in enumerate(MATMUL_SIZES):
                mh = (_select_rows(m, q, k) * jnp.exp2(_level_log_decay(b, m))).astype(BF16)
                if 2 * m == CHUNK:
                    top = lax.dot_general(mh[m:], mh[:m], dn, preferred_element_type=F32)
                    att = att + jnp.pad(top, ((m, 0), (0, m)))
                else:
                    att = att + lax.dot_general(mh, mh, dn, preferred_element_type=F32) * pair_ref[i]
            vb = proj(r0, CHUNK, 2 * HG_WIDTH + hd * HG_DK, 2 * HG_WIDTH + (hd + 1) * HG_DK)
            per_head.append((att.astype(BF16), vb.astype(BF16), (q * decay_in).astype(BF16),
                             (k * decay_out).astype(BF16), decay_in[CHUNK - 1:CHUNK, :]))
        return r0, per_head

    def hg_output(scored):
        r0, per_head = scored
        dn = (((1,), (1,)), ((), ()))
        outs = []
        for hd, (att, vb, q_in, k_out, decay_last) in enumerate(per_head):
            cs = slice(hd * HG_DK, (hd + 1) * HG_DK)
            st = st_ref[hd]
            o = (jnp.dot(att, vb, preferred_element_type=F32)
                 + lax.dot_general(q_in, st.astype(BF16), dn, preferred_element_type=F32))
            inc = lax.dot_general(vb, k_out, (((0,), (0,)), ((), ())),
                                  preferred_element_type=F32)
            st_ref[hd] = st * decay_last + inc
            half_g = 0.5 * proj(r0, CHUNK, 3 * HG_WIDTH + hd * HG_DK, 3 * HG_WIDTH + (hd + 1) * HG_DK)
            g_act = half_g + half_g * jnp.tanh(half_g)
            o = _rms(o, RMS_EPS) * gn_ref[:, cs] * g_act
            outs.append(o.astype(BF16))
        return jnp.concatenate(outs, axis=1)

    n_sub = MIX_ROWS // SUB_ROWS
    n_chunks = MIX_ROWS // CHUNK
    chunks_per_sub = SUB_ROWS // CHUNK
    chunks_per_blk = GM_BLOCK // CHUNK
    blocks_per_sub = SUB_ROWS // GM_BLOCK
    in_piece = IN_COLS // chunks_per_sub
    out_piece = D_MODEL // chunks_per_sub
    lag_out = 2

    y = [normalize(s * SUB_ROWS) for s in range(n_sub)]
    project(0, y[0], 0, IN_COLS)
    gates, scores, gm_state, hg_out, gm_out, cat = {}, {}, {}, {}, {}, {}

    def big_projections(slot):
        s_next = slot // chunks_per_sub + 1
        if s_next < n_sub:
            p = slot % chunks_per_sub
            project(s_next * SUB_ROWS, y[s_next], p * in_piece, (p + 1) * in_piece)
        done = slot - lag_out - chunks_per_sub
        if done >= 0 and done // chunks_per_sub < n_sub:
            s_done, p = divmod(done, chunks_per_sub)
            if p == 0:
                c0 = s_done * chunks_per_sub
                hg_part = jnp.concatenate([hg_out.pop(c0 + i) for i in range(chunks_per_sub)], axis=0)
                gm_part = jnp.concatenate(
                    [gm_out.pop(c0 + i) for i in range(0, chunks_per_sub, chunks_per_blk)], axis=0)
                cat[s_done] = jnp.concatenate([hg_part, gm_part], axis=1)
            out_project(s_done * SUB_ROWS, cat[s_done], p * out_piece, (p + 1) * out_piece)

    for slot in range(n_chunks + lag_out + chunks_per_sub):
        big_projections(slot)
        if 0 <= slot - lag_out < n_chunks:
            hg_out[slot - lag_out] = hg_output(scores.pop(slot - lag_out))
        if 0 <= slot - 1 < n_chunks:
            scores[slot - 1] = hg_scores(gates.pop(slot - 1))
        if slot < n_chunks:
            gates[slot] = hg_gates(slot * CHUNK)
            p = slot % chunks_per_sub
            if p < 2:
                head_ids = range(p * GM_HEADS // 2, (p + 1) * GM_HEADS // 2)
                for i, res in enumerate(gm_mix((slot - p) * CHUNK, blocks_per_sub, head_ids)):
                    gm_state.setdefault(slot - p + i * chunks_per_blk, []).extend(res)
            else:
                key = slot - p + (p - 2) * chunks_per_blk
                gm_out[key] = gm_gate(gm_state.pop(key))


def _mixer_call(h, mod, ngain, w_in, w_out, lb_logits, gnorm, ln_gain, out_gain, w_sp, b_sp,
                pair_c, *, layer):
    batch = h.shape[0]
    depth = lb_logits.shape[0]
    kern = functools.partial(_mixer_kernel, layer=layer)
    return pl.pallas_call(
        kern,
        grid=(batch, SEQ // MIX_ROWS),
        in_specs=[
            pl.BlockSpec((1, MIX_ROWS, D_MODEL), lambda b, j: (b, j, 0)),
            pl.BlockSpec((1, N_MOD, D_MODEL), lambda b, j: (b, 0, 0)),
            _resident((1, D_MODEL)),
            pl.BlockSpec(memory_space=pl.ANY),
            pl.BlockSpec(memory_space=pl.ANY),
            _resident((depth, HG_WIDTH)),
            _resident((1, HG_WIDTH)),
            _resident((1, GM_WIDTH)),
            _resident((1, GM_WIDTH)),
            _resident((GM_HEADS, GM_BLOCK, GM_BLOCK)),
            _resident((GM_HEADS, GM_BLOCK, 1)),
            _resident(pair_c.shape),
        ],
        out_specs=pl.BlockSpec((1, MIX_ROWS, D_MODEL), lambda b, j: (b, j, 0)),
        out_shape=jax.ShapeDtypeStruct(h.shape, F32),
        scratch_shapes=[pltpu.VMEM((HG_HEADS, HG_DK, HG_DK), F32),
                        pltpu.VMEM((MIX_ROWS, IN_COLS), F32),
                        pltpu.VMEM((IN_COLS // WIN_BLOCK, D_MODEL, WIN_BLOCK), BF16),
                        pltpu.VMEM((D_MODEL // WOUT_BLOCK, D_MODEL, WOUT_BLOCK), BF16),
                        pltpu.VMEM((2, D_MODEL, WIN_BLOCK), F32),
                        pltpu.VMEM((2, D_MODEL, WOUT_BLOCK), F32),
                        pltpu.SemaphoreType.DMA((2,)),
                        pltpu.SemaphoreType.DMA((2,))],
        compiler_params=pltpu.CompilerParams(
            dimension_semantics=("arbitrary", "arbitrary"), vmem_limit_bytes=VMEM_LIMIT),
        name="mixer",
    )(h, mod, ngain, w_in, w_out, lb_logits, gnorm, ln_gain, out_gain, w_sp, b_sp, pair_c)


def kernel(x, c, w_ada, b_ada, norm_gain, ffn1_w13, ffn1_w2, w_in, hg_lb_logits, hg_gnorm,
           gm_ln_gain, gm_w_spatial, gm_b_spatial, gm_out_gain, w_out, ffn2_w13, ffn2_w2,
           final_gain):
    batch, seq, _ = x.shape
    depth = w_ada.shape[0]
    pair_c = jnp.asarray(_pair_masks(), F32)
    fgain = final_gain.reshape(1, D_MODEL)

    mod_all = _ada_call(c, w_ada, b_ada).reshape(depth, batch, N_MOD, D_MODEL)
    h = x
    for l in range(depth):
        mod = mod_all[l]
        h = _ffn_call(h.reshape(batch * seq, D_MODEL), mod, norm_gain[l, 0:1], ffn1_w13, ffn1_w2,
                      fgain, layer=l, mod_base=0, final=False).reshape(batch, seq, D_MODEL)
        h = _mixer_call(h, mod, norm_gain[l, 1:2], w_in, w_out,
                        hg_lb_logits, hg_gnorm[l:l + 1], gm_ln_gain[l:l + 1], gm_out_gain[l:l + 1],
                        gm_w_spatial[l], gm_b_spatial[l].reshape(GM_HEADS, GM_BLOCK, 1),
                        pair_c, layer=l)
        h = _ffn_call(h.reshape(batch * seq, D_MODEL), mod, norm_gain[l, 2:3], ffn2_w13, ffn2_w2,
                      fgain, layer=l, mod_base=6,
                      final=(l == depth - 1)).reshape(batch, seq, D_MODEL)
    return h
```
